```python
import math
import jax, jax.numpy as jnp
from jax import lax
import numpy as np

D_MODEL = 1024
BATCH = 8
SEQ = 2048
DEPTH = 4

N_MIXERS = 2
N_A_LAYERS = (DEPTH + N_MIXERS - 1) // N_MIXERS
N_B_LAYERS = DEPTH // N_MIXERS

SSM_EXPAND = 2
D_INNER = SSM_EXPAND * D_MODEL
HEAD_DIM = 64
N_HEADS = D_INNER // HEAD_DIM
N_GROUPS = 4
HEADS_PER_GROUP = N_HEADS // N_GROUPS
D_STATE = 128
SSM_CONV = 4
CONV_DIM = D_INNER + 2 * N_GROUPS * D_STATE
D_IN_PROJ = 2 * D_INNER + 2 * N_GROUPS * D_STATE + N_HEADS
CHUNK = 128
DT_MIN = 0.001
DT_MAX = 0.1

CONV_KERNEL = 31

D_FF = -(-8 * D_MODEL // (3 * 256)) * 256

EPS = 1e-5

kernel_name = "hybrid_ssd_conformer_trunk"


def rmsnorm(x, g):
    xf = x.astype(jnp.float32)
    y = xf * lax.rsqrt(jnp.mean(xf * xf, axis=-1, keepdims=True) + EPS)
    return (y * g).astype(x.dtype)


def layernorm(x, g, b):
    xf = x.astype(jnp.float32)
    mu = jnp.mean(xf, axis=-1, keepdims=True)
    xc = xf - mu
    var = jnp.mean(xc * xc, axis=-1, keepdims=True)
    return (xc * lax.rsqrt(var + EPS) * g + b).astype(x.dtype)


def gated_group_rmsnorm(y, z, g):
    h = (y * jax.nn.silu(z)).astype(jnp.float32)
    shp = h.shape
    h = h.reshape(shp[:-1] + (N_GROUPS, shp[-1] // N_GROUPS))
    h = h * lax.rsqrt(jnp.mean(h * h, axis=-1, keepdims=True) + EPS)
    return (h.reshape(shp) * g).astype(y.dtype)


def causal_depthwise_conv(x, w, b):
    k = w.shape[0]
    y = lax.conv_general_dilated(
        x, w[:, None, :], window_strides=(1,), padding=[(k - 1, 0)],
        dimension_numbers=("NWC", "WIO", "NWC"), feature_group_count=x.shape[-1])
    return y + b


def segsum(a):
    t = a.shape[-1]
    ar = jnp.broadcast_to(a[..., :, None], a.shape + (t,))
    ar = jnp.where(jnp.tril(jnp.ones((t, t), dtype=bool), -1), ar, 0.0)
    cs = jnp.cumsum(ar, axis=-2)
    return jnp.where(jnp.tril(jnp.ones((t, t), dtype=bool), 0), cs, -jnp.inf)


def ssd_chunked(x, da, bm, cm):
    b, l, h, p = x.shape
    c = l // CHUNK
    dt_ = x.dtype
    x = x.reshape(b, c, CHUNK, N_GROUPS, HEADS_PER_GROUP, p)
    bm = bm.reshape(b, c, CHUNK, N_GROUPS, D_STATE)
    cm = cm.reshape(b, c, CHUNK, N_GROUPS, D_STATE)
    a = da.reshape(b, c, CHUNK, N_GROUPS, HEADS_PER_GROUP).transpose(0, 3, 4, 1, 2)
    a_cs = jnp.cumsum(a, axis=-1)

    lmat = jnp.exp(segsum(a)).astype(dt_)
    cb = jnp.einsum("bclgn,bcsgn->bgcls", cm, bm)
    y_diag = jnp.einsum("bgrcls,bcsgrp->bclgrp", cb[:, :, None] * lmat, x)

    decay_states = jnp.exp(a_cs[..., -1:] - a_cs).astype(dt_)
    xd = x * decay_states.transpose(0, 3, 4, 1, 2)[..., None]
    states = jnp.einsum("bclgn,bclgrp->bcgrpn", bm, xd)

    chunk_tot = jnp.pad(a_cs[..., -1], ((0, 0), (0, 0), (0, 0), (1, 0)))
    decay_chunk = jnp.exp(segsum(chunk_tot))[..., :c, 1:].astype(dt_)
    states_in = jnp.einsum("bgrzj,bjgrpn->bzgrpn", decay_chunk, states)

    state_decay_out = jnp.exp(a_cs).astype(dt_).transpose(0, 3, 4, 1, 2)
    y_off = jnp.einsum("bclgn,bcgrpn->bclgrp", cm, states_in) * state_decay_out[..., None]
    return (y_diag + y_off).reshape(b, l, h, p)


def mamba2_mixer(u, w_in, conv_w, conv_b, dt_bias, a_log, d_skip, gate_norm, w_out):
    b, l, _ = u.shape
    zxbcdt = u @ w_in
    z, xbc, dt = jnp.split(zxbcdt, [D_INNER, D_INNER + CONV_DIM], axis=-1)
    xbc = jax.nn.silu(causal_depthwise_conv(xbc, conv_w, conv_b))
    xs, bm, cm = jnp.split(xbc, [D_INNER, D_INNER + N_GROUPS * D_STATE], axis=-1)
    xs = xs.reshape(b, l, N_HEADS, HEAD_DIM)
    bm = bm.reshape(b, l, N_GROUPS, D_STATE)
    cm = cm.reshape(b, l, N_GROUPS, D_STATE)
    dt = jax.nn.softplus(dt.astype(jnp.float32) + dt_bias.astype(jnp.float32))
    a = -jnp.exp(a_log.astype(jnp.float32))
    y = ssd_chunked(xs * dt[..., None].astype(xs.dtype), dt * a, bm, cm)
    y = y + xs * d_skip[:, None]
    y = gated_group_rmsnorm(y.reshape(b, l, D_INNER), z, gate_norm)
    return y @ w_out


def conformer_conv_module(u, w_pw1, b_pw1, dw_w, dw_b, ln_g, ln_b, w_pw2, b_pw2):
    h = u @ w_pw1 + b_pw1
    a, g = jnp.split(h, 2, axis=-1)
    h = a * jax.nn.sigmoid(g)
    h = causal_depthwise_conv(h, dw_w, dw_b)
    h = jax.nn.silu(layernorm(h, ln_g, ln_b))
    return h @ w_pw2 + b_pw2


def swiglu_ffn(u, w_gate, w_up, w_down):
    return (jax.nn.silu(u @ w_gate) * (u @ w_up)) @ w_down


def setup_inputs(seed: int = 0) -> dict:
    key = jax.random.key(seed)
    ks = iter(jax.random.split(key, 40))

    def nrm(shape, scale):
        return jax.random.normal(next(ks), shape, jnp.float32) * scale

    na, nb = N_A_LAYERS, N_B_LAYERS
    x = nrm((BATCH, SEQ, D_MODEL), 1.0)

    u = jax.random.uniform(next(ks), (na, N_HEADS), jnp.float32)
    dt0 = jnp.exp(u * (math.log(DT_MAX) - math.log(DT_MIN)) + math.log(DT_MIN))
    dt0 = jnp.maximum(dt0, 1e-4)
    ssm_dt_bias = dt0 + jnp.log(-jnp.expm1(-dt0))
    ssm_a_log = jnp.log(jax.random.uniform(next(ks), (na, N_HEADS), jnp.float32, 1.0, 16.0))

    return {
        "x": x,
        "ssm_norm": 1.0 + nrm((na, D_MODEL), 0.05),
        "ssm_w_in": nrm((na, D_MODEL, D_IN_PROJ), D_MODEL ** -0.5),
        "ssm_conv_w": nrm((na, SSM_CONV, CONV_DIM), SSM_CONV ** -0.5),
        "ssm_conv_b": nrm((na, CONV_DIM), 0.02),
        "ssm_dt_bias": ssm_dt_bias,
        "ssm_a_log": ssm_a_log,
        "ssm_d": 1.0 + nrm((na, N_HEADS), 0.1),
        "ssm_gate_norm": 1.0 + nrm((na, D_INNER), 0.05),
        "ssm_w_out": nrm((na, D_INNER, D_MODEL), D_INNER ** -0.5),
        "cv_norm": 1.0 + nrm((nb, D_MODEL), 0.05),
        "cv_w_pw1": nrm((nb, D_MODEL, 2 * D_MODEL), D_MODEL ** -0.5),
        "cv_b_pw1": nrm((nb, 2 * D_MODEL), 0.02),
        "cv_dw_w": nrm((nb, CONV_KERNEL, D_MODEL), CONV_KERNEL ** -0.5),
        "cv_dw_b": nrm((nb, D_MODEL), 0.02),
        "cv_ln_g": 1.0 + nrm((nb, D_MODEL), 0.05),
        "cv_ln_b": nrm((nb, D_MODEL), 0.02),
        "cv_w_pw2": nrm((nb, D_MODEL, D_MODEL), D_MODEL ** -0.5),
        "cv_b_pw2": nrm((nb, D_MODEL), 0.02),
        "ffn_norm": 1.0 + nrm((DEPTH, D_MODEL), 0.05),
        "ffn_w_gate": nrm((DEPTH, D_MODEL, D_FF), D_MODEL ** -0.5),
        "ffn_w_up": nrm((DEPTH, D_MODEL, D_FF), D_MODEL ** -0.5),
        "ffn_w_down": nrm((DEPTH, D_FF, D_MODEL), D_FF ** -0.5),
        "final_norm": 1.0 + nrm((D_MODEL,), 0.05),
    }


def reference(x, ssm_norm, ssm_w_in, ssm_conv_w, ssm_conv_b, ssm_dt_bias, ssm_a_log, ssm_d,
              ssm_gate_norm, ssm_w_out, cv_norm, cv_w_pw1, cv_b_pw1, cv_dw_w, cv_dw_b, cv_ln_g,
              cv_ln_b, cv_w_pw2, cv_b_pw2, ffn_norm, ffn_w_gate, ffn_w_up, ffn_w_down, final_norm):
    h = x
    for i in range(DEPTH):
        j = i // N_MIXERS
        if i % N_MIXERS == 0:
            h = h + mamba2_mixer(rmsnorm(h, ssm_norm[j]), ssm_w_in[j], ssm_conv_w[j], ssm_conv_b[j],
                                 ssm_dt_bias[j], ssm_a_log[j], ssm_d[j], ssm_gate_norm[j], ssm_w_out[j])
        else:
            h = h + conformer_conv_module(rmsnorm(h, cv_norm[j]), cv_w_pw1[j], cv_b_pw1[j], cv_dw_w[j],
                                          cv_dw_b[j], cv_ln_g[j], cv_ln_b[j], cv_w_pw2[j], cv_b_pw2[j])
        h = h + swiglu_ffn(rmsnorm(h, ffn_norm[i]), ffn_w_gate[i], ffn_w_up[i], ffn_w_down[i])
    return rmsnorm(h, final_norm)
```

```python
import functools

import numpy as np
import jax
import jax.numpy as jnp
from jax import lax
from jax.experimental import pallas as pl
from jax.experimental.pallas import tpu as pltpu

D_MODEL = 1024
DEPTH = 4
D_INNER = 2048
HEAD_DIM = 64
N_HEADS = 32
N_GROUPS = 4
HEADS_PER_GROUP = 8
D_STATE = 128
SSM_CONV = 4
CONV_DIM = D_INNER + 2 * N_GROUPS * D_STATE
CHUNK = 128
CONV_KERNEL = 31
D_FF = 2816
EPS = 1e-5

V7X_LANES = 128
V7X_SUBLANES = 8
V7X_VMEM_LIMIT_BYTES = 56 * 1024 * 1024

GROUP_WIDTH = HEADS_PER_GROUP * HEAD_DIM
HEAD_REPLICAS = 3
DIAG_HEADS = 4
MASKED_LOG = -1e30

F32 = jnp.float32
BF16 = jnp.bfloat16


def _sigmoid(x):
    return 1.0 / (1.0 + jnp.exp(-x))


def _silu(x):
    return x * _sigmoid(x)


def _rms_scale(x, g):
    ms = jnp.mean(x * x, axis=-1, keepdims=True)
    return x * lax.rsqrt(ms + EPS) * g


def _dot(a, b):
    return jnp.dot(a, b, preferred_element_type=F32)


def _resident(shape):
    nd = len(shape)
    return pl.BlockSpec(shape, lambda *_: (0,) * nd, pipeline_mode=pl.Buffered(1))


def _params(n_grid):
    return pltpu.CompilerParams(
        dimension_semantics=("arbitrary",) * n_grid,
        vmem_limit_bytes=V7X_VMEM_LIMIT_BYTES)


def _ffn_kernel(h_ref, g_ref, wg_ref, wu_ref, wd_ref, fg_ref, o_ref, *, final_norm):
    x = h_ref[...]
    xn = _rms_scale(x, g_ref[...]).astype(BF16)
    gate = _dot(xn, wg_ref[...])
    up = _dot(xn, wu_ref[...])
    act = (_silu(gate) * up).astype(BF16)
    y = x + _dot(act, wd_ref[...])
    if final_norm:
        y = _rms_scale(y, fg_ref[...])
    o_ref[...] = y


def _ffn_layer(h, g, wg, wu, wd, fg, *, final_norm, tm):
    b, l, d = h.shape
    tile = pl.BlockSpec((None, tm, d), lambda i, j: (i, j, 0))
    return pl.pallas_call(
        functools.partial(_ffn_kernel, final_norm=final_norm),
        grid=(b, l // tm),
        in_specs=[tile, _resident(g.shape), _resident(wg.shape), _resident(wu.shape),
                  _resident(wd.shape), _resident(fg.shape)],
        out_specs=tile,
        out_shape=jax.ShapeDtypeStruct(h.shape, h.dtype),
        compiler_params=_params(2),
        name="swiglu_ffn",
    )(h, g, wg, wu, wd, fg)


CONV_HALO = 32


def _conf_kernel(h_ref, g_ref, w1_ref, b1_ref, dww_ref, dwb_ref, lng_ref, lnb_ref,
                 w2_ref, b2_ref, o_ref, buf_ref, *, tm):
    @pl.when(pl.program_id(1) == 0)
    def _():
        buf_ref[0:CONV_HALO, :] = jnp.zeros((CONV_HALO, D_MODEL), F32)

    x = h_ref[...]
    xn = _rms_scale(x, g_ref[...]).astype(BF16)
    hh = _dot(xn, w1_ref[...]) + b1_ref[...]
    buf_ref[CONV_HALO:CONV_HALO + tm, :] = hh[:, :D_MODEL] * _sigmoid(hh[:, D_MODEL:])

    base = CONV_HALO - (CONV_KERNEL - 1)
    acc = jnp.broadcast_to(dwb_ref[...], (tm, D_MODEL))
    for k in range(CONV_KERNEL):
        acc = acc + dww_ref[k:k + 1, :] * buf_ref[pl.ds(base + k, tm), :]
    buf_ref[0:CONV_HALO, :] = buf_ref[tm:tm + CONV_HALO, :]

    mu = jnp.mean(acc, axis=-1, keepdims=True)
    xc = acc - mu
    var = jnp.mean(xc * xc, axis=-1, keepdims=True)
    y = xc * lax.rsqrt(var + EPS) * lng_ref[...] + lnb_ref[...]
    s = _silu(y).astype(BF16)
    o_ref[...] = x + _dot(s, w2_ref[...]) + b2_ref[...]


def _conf_layer(h, g, w1, b1, dww, dwb, lng, lnb, w2, b2, *, tm):
    b, l, d = h.shape
    tile = pl.BlockSpec((None, tm, d), lambda i, j: (i, j, 0))
    args = (g, w1, b1, dww, dwb, lng, lnb, w2, b2)
    return pl.pallas_call(
        functools.partial(_conf_kernel, tm=tm),
        grid=(b, l // tm),
        in_specs=[tile] + [_resident(a.shape) for a in args],
        out_specs=tile,
        out_shape=jax.ShapeDtypeStruct(h.shape, h.dtype),
        scratch_shapes=[pltpu.VMEM((CONV_HALO + tm, d), F32)],
        compiler_params=_params(2),
        name="conformer_conv",
    )(h, *args)


SSM_HALO = V7X_SUBLANES
DT_WIDTH = V7X_LANES


def _split_pack(v, lane):
    hi = v.astype(BF16)
    r1 = v - hi.astype(F32)
    mid = r1.astype(BF16)
    lo = (r1 - mid.astype(F32)).astype(BF16)
    zero = jnp.zeros_like(hi)
    return jnp.where(lane < N_HEADS, hi,
                     jnp.where(lane < 2 * N_HEADS, mid,
                               jnp.where(lane < 3 * N_HEADS, lo, zero)))


def _ssm_kernel(h_ref, g_ref, wzx_ref, wdt_ref, cw_ref, cb_ref, dtb_ref, alog_ref, dsk_ref,
                gn_ref, wout_ref, e64_ref, e128_ref, o_ref, buf_ref, st_ref, *, tm):
    @pl.when(pl.program_id(1) == 0)
    def _():
        buf_ref[0:SSM_HALO, :] = jnp.zeros((SSM_HALO, CONV_DIM), F32)
        st_ref[...] = jnp.zeros(st_ref.shape, F32)

    x = h_ref[...]
    xn = _rms_scale(x, g_ref[...]).astype(BF16)
    zx = _dot(xn, wzx_ref[...])
    z = zx[:, :D_INNER]
    buf_ref[SSM_HALO:SSM_HALO + tm, :] = zx[:, D_INNER:]

    base = SSM_HALO - (SSM_CONV - 1)
    conv = jnp.broadcast_to(cb_ref[...], (tm, CONV_DIM))
    for k in range(SSM_CONV):
        conv = conv + cw_ref[k:k + 1, :] * buf_ref[pl.ds(base + k, tm), :]
    buf_ref[0:SSM_HALO, :] = buf_ref[tm:tm + SSM_HALO, :]
    xbc = _silu(conv)
    xs = xbc[:, :D_INNER]
    bm = xbc[:, D_INNER:D_INNER + N_GROUPS * D_STATE]
    cm = xbc[:, D_INNER + N_GROUPS * D_STATE:]

    dtr = _dot(xn, wdt_ref[...]) + dtb_ref[...]
    dt = jnp.maximum(dtr, 0.0) + jnp.log(1.0 + jnp.exp(-jnp.abs(dtr)))
    a = dt * (-jnp.exp(alog_ref[...]))

    lane = lax.broadcasted_iota(jnp.int32, (CHUNK, DT_WIDTH), 1)
    row = lax.broadcasted_iota(jnp.int32, (CHUNK, CHUNK), 0)
    col = lax.broadcasted_iota(jnp.int32, (CHUNK, CHUNK), 1)
    causal = col <= row
    e64 = e64_ref[...]
    e128 = e128_ref[...]
    diag_rows = DIAG_HEADS * CHUNK
    diag_cols = DIAG_HEADS * HEAD_DIM
    blk_r = lax.broadcasted_iota(jnp.int32, (diag_rows, diag_cols), 0) // CHUNK
    blk_c = lax.broadcasted_iota(jnp.int32, (diag_rows, diag_cols), 1) // HEAD_DIM
    on_diag = blk_r == blk_c

    y_chunks = []
    for q in range(tm // CHUNK):
        r0 = q * CHUNK
        a_q = a[r0:r0 + CHUNK, :]
        acs = a_q
        sh = 1
        while sh < CHUNK:
            shifted = pltpu.roll(acs, sh, 0)
            acs = acs + jnp.where(row >= sh, shifted, 0.0)
            sh *= 2
        acs_t = acs.T
        acs64 = _dot(_split_pack(acs, lane), e64)
        acs128 = _dot(_split_pack(acs, lane), e128)
        dt64 = _dot(_split_pack(dt[r0:r0 + CHUNK, :], lane), e64)
        last64 = acs64[CHUNK - 1:CHUNK, :]
        xs_q = xs[r0:r0 + CHUNK, :]
        xdt = xs_q * dt64
        xdt_b = xdt.astype(BF16)
        xdec = (xdt * jnp.exp(last64 - acs64)).astype(BF16)
        dec_out = jnp.exp(acs64)
        dec_tot = jnp.exp(last64)

        y_groups = []
        for g in range(N_GROUPS):
            gs = slice(g * GROUP_WIDTH, (g + 1) * GROUP_WIDTH)
            b_g = bm[r0:r0 + CHUNK, g * D_STATE:(g + 1) * D_STATE]
            c_g = cm[r0:r0 + CHUNK, g * D_STATE:(g + 1) * D_STATE].astype(BF16)
            b_gt = b_g.T.astype(BF16)
            cb = _dot(c_g, b_gt)
            st = st_ref[g]
            y_g = _dot(c_g, st.astype(BF16)) * dec_out[:, gs]
            st_ref[g] = st * dec_tot[:, gs] + _dot(b_gt, xdec[:, gs])

            y_diag = []
            for hb in range(HEADS_PER_GROUP // DIAG_HEADS):
                h0 = g * HEADS_PER_GROUP + hb * DIAG_HEADS
                ms = []
                for r in range(DIAG_HEADS):
                    h = h0 + r
                    seg = acs128[:, h * CHUNK:(h + 1) * CHUNK] - acs_t[h:h + 1, :]
                    lmat = jnp.exp(jnp.where(causal, seg, MASKED_LOG))
                    ms.append((cb * lmat).astype(BF16))
                m_cat = jnp.concatenate(ms, axis=1)
                x_blk = xdt_b[:, h0 * HEAD_DIM:(h0 + DIAG_HEADS) * HEAD_DIM]
                x_rep = jnp.concatenate([x_blk] * DIAG_HEADS, axis=0)
                x_diag = jnp.where(on_diag, x_rep, jnp.zeros_like(x_rep))
                y_diag.append(_dot(m_cat, x_diag))
            y_groups.append(y_g + jnp.concatenate(y_diag, axis=1))
        y_chunks.append(jnp.concatenate(y_groups, axis=1) + xs_q * dsk_ref[...])

    y = jnp.concatenate(y_chunks, axis=0) if len(y_chunks) > 1 else y_chunks[0]
    hg = y * _silu(z)
    normed = []
    for g in range(N_GROUPS):
        gs = slice(g * GROUP_WIDTH, (g + 1) * GROUP_WIDTH)
        hs = hg[:, gs]
        ms = jnp.mean(hs * hs, axis=-1, keepdims=True)
        normed.append(hs * lax.rsqrt(ms + EPS))
    hn = (jnp.concatenate(normed, axis=1) * gn_ref[...]).astype(BF16)
    o_ref[...] = x + _dot(hn, wout_ref[...])


def _expander(width):
    e = np.zeros((DT_WIDTH, N_HEADS * width), np.float32)
    for r in range(HEAD_REPLICAS):
        for h in range(N_HEADS):
            e[r * N_HEADS + h, h * width:(h + 1) * width] = 1.0
    return jnp.asarray(e, BF16)


def _replicate_heads(v):
    rep = jnp.tile(v, HEAD_REPLICAS)
    return jnp.pad(rep, (0, DT_WIDTH - rep.shape[0]))[None, :]


def _ssm_layer(h, g, w_in, conv_w, conv_b, dt_bias, a_log, d_skip, gate_norm, w_out, *, tm):
    b, l, d = h.shape
    n_zx = D_INNER + CONV_DIM
    wzx = w_in[:, :n_zx].astype(BF16)
    wdt = w_in[:, n_zx:]
    wdt = jnp.pad(jnp.tile(wdt, (1, HEAD_REPLICAS)),
                  ((0, 0), (0, DT_WIDTH - HEAD_REPLICAS * N_HEADS))).astype(BF16)
    args = (g[None, :], wzx, wdt, conv_w, conv_b[None, :], _replicate_heads(dt_bias),
            _replicate_heads(a_log), jnp.repeat(d_skip, HEAD_DIM)[None, :], gate_norm[None, :],
            w_out.astype(BF16), _expander(HEAD_DIM), _expander(CHUNK))
    tile = pl.BlockSpec((None, tm, d), lambda i, j: (i, j, 0))
    return pl.pallas_call(
        functools.partial(_ssm_kernel, tm=tm),
        grid=(b, l // tm),
        in_specs=[tile] + [_resident(a.shape) for a in args],
        out_specs=tile,
        out_shape=jax.ShapeDtypeStruct(h.shape, h.dtype),
        scratch_shapes=[pltpu.VMEM((SSM_HALO + tm, CONV_DIM), F32),
                        pltpu.VMEM((N_GROUPS, D_STATE, GROUP_WIDTH), F32)],
        compiler_params=_params(2),
        name="ssd_mixer",
    )(h, *args)


SSM_TILE = 256
CONF_TILE = 512
FFN_TILE = 512


def kernel(x, ssm_norm, ssm_w_in, ssm_conv_w, ssm_conv_b, ssm_dt_bias, ssm_a_log, ssm_d, ssm_gate_norm, ssm_w_out, cv_norm, cv_w_pw1, cv_b_pw1, cv_dw_w, cv_dw_b, cv_ln_g, cv_ln_b, cv_w_pw2, cv_b_pw2, ffn_norm, ffn_w_gate, ffn_w_up, ffn_w_down, final_norm):
    h = x
    for i in range(DEPTH):
        j = i // 2
        if i % 2 == 0:
            h = _ssm_layer(h, ssm_norm[j], ssm_w_in[j], ssm_conv_w[j], ssm_conv_b[j],
                           ssm_dt_bias[j], ssm_a_log[j], ssm_d[j], ssm_gate_norm[j],
                           ssm_w_out[j], tm=SSM_TILE)
        else:
            h = _conf_layer(h, cv_norm[j][None, :], cv_w_pw1[j].astype(BF16),
                            cv_b_pw1[j][None, :], cv_dw_w[j], cv_dw_b[j][None, :],
                            cv_ln_g[j][None, :], cv_ln_b[j][None, :],
                            cv_w_pw2[j].astype(BF16), cv_b_pw2[j][None, :], tm=CONF_TILE)
        h = _ffn_layer(h, ffn_norm[i][None, :], ffn_w_gate[i].astype(BF16),
                       ffn_w_up[i].astype(BF16), ffn_w_down[i].astype(BF16),
                       final_norm[None, :], final_norm=(i == DEPTH - 1), tm=FFN_TILE)
    return h
```

```python
import functools

import numpy as np
import jax
import jax.numpy as jnp
from jax import lax
from jax.experimental import pallas as pl
from jax.experimental.pallas import tpu as pltpu

D_MODEL = 1024
DEPTH = 4
D_INNER = 2048
HEAD_DIM = 64
N_HEADS = 32
N_GROUPS = 4
HEADS_PER_GROUP = 8
D_STATE = 128
SSM_CONV = 4
CONV_DIM = D_INNER + 2 * N_GROUPS * D_STATE
CHUNK = 128
CONV_KERNEL = 31
D_FF = 2816
EPS = 1e-5

V7X_LANES = 128
V7X_SUBLANES = 8
V7X_VMEM_LIMIT_BYTES = 56 * 1024 * 1024

ROW_GROUPS = CHUNK // V7X_SUBLANES
GROUP_WIDTH = HEADS_PER_GROUP * HEAD_DIM
HEAD_REPLICAS = 3
PAIR_WIDTH = 2 * HEAD_DIM
PAIRS_PER_GROUP = HEADS_PER_GROUP // 2
HALF_CHUNK = CHUNK // 2
MASKED_LOG = -1e30
LANE_BLOCKS = D_MODEL // V7X_LANES

F32 = jnp.float32
BF16 = jnp.bfloat16


def _sigmoid(x):
    return 1.0 / (1.0 + jnp.exp(-x))


def _silu(x):
    return x * _sigmoid(x)


def _rms_scale(x, g):
    ms = jnp.mean(x * x, axis=-1, keepdims=True)
    return x * lax.rsqrt(ms + EPS) * g


def _dot(a, b):
    return jnp.dot(a, b, preferred_element_type=F32)


def _resident(shape):
    nd = len(shape)
    return pl.BlockSpec(shape, lambda *_: (0,) * nd, pipeline_mode=pl.Buffered(1))


def _params(n_grid):
    return pltpu.CompilerParams(
        dimension_semantics=("arbitrary",) * n_grid,
        vmem_limit_bytes=V7X_VMEM_LIMIT_BYTES)


def _interleave(x):
    b, l, d = x.shape
    x = x.reshape(b, l // CHUNK, V7X_SUBLANES, ROW_GROUPS, d)
    return x.swapaxes(2, 3).reshape(b, l, d)


def _deinterleave(x):
    b, l, d = x.shape
    x = x.reshape(b, l // CHUNK, ROW_GROUPS, V7X_SUBLANES, d)
    return x.swapaxes(2, 3).reshape(b, l, d)


def _shift_down(x, j, sub):
    return jnp.where(sub >= j, pltpu.roll(x, j, 0), 0.0)


def _ffn_kernel(h_ref, g_ref, wg_ref, wu_ref, wd_ref, fg_ref, o_ref, *, final_norm):
    x = h_ref[...]
    xn = _rms_scale(x, g_ref[...]).astype(BF16)
    gate = _dot(xn, wg_ref[...])
    up = _dot(xn, wu_ref[...])
    act = (_silu(gate) * up).astype(BF16)
    y = x + _dot(act, wd_ref[...])
    if final_norm:
        y = _rms_scale(y, fg_ref[...])
    o_ref[...] = y


def _ffn_layer(h, g, wg, wu, wd, fg, *, final_norm, tm):
    b, l, d = h.shape
    tile = pl.BlockSpec((None, tm, d), lambda i, j: (i, j, 0))
    return pl.pallas_call(
        functools.partial(_ffn_kernel, final_norm=final_norm),
        grid=(b, l // tm),
        in_specs=[tile, _resident(g.shape), _resident(wg.shape), _resident(wu.shape),
                  _resident(wd.shape), _resident(fg.shape)],
        out_specs=tile,
        out_shape=jax.ShapeDtypeStruct(h.shape, h.dtype),
        compiler_params=_params(2),
        name="swiglu_ffn",
    )(h, g, wg, wu, wd, fg)


def _conf_kernel(h_ref, g_ref, w1_ref, b1_ref, dww_ref, dwb_ref, lng_ref, lnb_ref,
                 w2_ref, b2_ref, o_ref, u0_ref, u1_ref, u2_ref, acc_ref, *, tm):
    nblk = tm // CHUNK

    @pl.when(pl.program_id(1) == 0)
    def _():
        u0_ref[:, 0:CHUNK, :] = jnp.zeros((LANE_BLOCKS, CHUNK, V7X_LANES), F32)

    x = h_ref[...]
    xn = _rms_scale(x, g_ref[...]).astype(BF16)
    hh = _dot(xn, w1_ref[...]) + b1_ref[...]
    u = hh[:, :D_MODEL] * _sigmoid(hh[:, D_MODEL:])
    for lb in range(LANE_BLOCKS):
        u0_ref[lb, CHUNK:CHUNK + tm, :] = u[:, lb * V7X_LANES:(lb + 1) * V7X_LANES]

    sub = lax.broadcasted_iota(jnp.int32, (V7X_SUBLANES, V7X_LANES), 0)

    def lane_block(lb, carry):
        def group(ref, c, v):
            r = c * CHUNK + v * V7X_SUBLANES
            return ref[lb, r:r + V7X_SUBLANES, :]

        prev1 = [pltpu.roll(group(u0_ref, 0, v), 1, 0) for v in range(ROW_GROUPS)]
        prev2 = [pltpu.roll(group(u0_ref, 0, v), 2, 0) for v in range(ROW_GROUPS)]
        for c in range(nblk):
            for v in range(ROW_GROUPS):
                cur = group(u0_ref, c + 1, v)
                r1 = pltpu.roll(cur, 1, 0)
                r2 = pltpu.roll(cur, 2, 0)
                r = c * CHUNK + v * V7X_SUBLANES
                u1_ref[lb, r:r + V7X_SUBLANES, :] = jnp.where(sub >= 1, r1, prev1[v])
                u2_ref[lb, r:r + V7X_SUBLANES, :] = jnp.where(sub >= 2, r2, prev2[v])
                prev1[v] = r1
                prev2[v] = r2

        w = [jnp.broadcast_to(dww_ref[lb, k:k + 1, :], (V7X_SUBLANES, V7X_LANES))
             for k in range(CONV_KERNEL)]
        bias = jnp.broadcast_to(dwb_ref[lb], (V7X_SUBLANES, V7X_LANES))
        for c in range(nblk):
            for v in range(ROW_GROUPS):
                acc = bias
                for d in range(CONV_KERNEL):
                    a, r = divmod(d, ROW_GROUPS)
                    vv, j = (v - r, a) if v >= r else (v - r + ROW_GROUPS, a + 1)
                    if j == 0:
                        src = group(u0_ref, c + 1, vv)
                    elif j == 1:
                        src = group(u1_ref, c, vv)
                    else:
                        src = group(u2_ref, c, vv)
                    acc = acc + w[CONV_KERNEL - 1 - d] * src
                r = c * CHUNK + v * V7X_SUBLANES
                acc_ref[lb, r:r + V7X_SUBLANES, :] = acc
        return carry

    lax.fori_loop(0, LANE_BLOCKS, lane_block, 0)
    u0_ref[:, 0:CHUNK, :] = u0_ref[:, tm:tm + CHUNK, :]

    conv = jnp.concatenate([acc_ref[lb] for lb in range(LANE_BLOCKS)], axis=1)
    mu = jnp.mean(conv, axis=-1, keepdims=True)
    xc = conv - mu
    var = jnp.mean(xc * xc, axis=-1, keepdims=True)
    y = xc * lax.rsqrt(var + EPS) * lng_ref[...] + lnb_ref[...]
    s = _silu(y).astype(BF16)
    o_ref[...] = x + _dot(s, w2_ref[...]) + b2_ref[...]


def _lane_blocked(w):
    return w.reshape(w.shape[0], LANE_BLOCKS, V7X_LANES).swapaxes(0, 1)


def _conf_layer(h, g, w1, b1, dww, dwb, lng, lnb, w2, b2, *, tm):
    b, l, d = h.shape
    tile = pl.BlockSpec((None, tm, d), lambda i, j: (i, j, 0))
    args = (g, w1, b1, _lane_blocked(dww), _lane_blocked(dwb), lng, lnb, w2, b2)
    blocked = functools.partial(pltpu.VMEM, dtype=F32)
    return pl.pallas_call(
        functools.partial(_conf_kernel, tm=tm),
        grid=(b, l // tm),
        in_specs=[tile] + [_resident(a.shape) for a in args],
        out_specs=tile,
        out_shape=jax.ShapeDtypeStruct(h.shape, h.dtype),
        scratch_shapes=[blocked((LANE_BLOCKS, CHUNK + tm, V7X_LANES)),
                        blocked((LANE_BLOCKS, tm, V7X_LANES)),
                        blocked((LANE_BLOCKS, tm, V7X_LANES)),
                        blocked((LANE_BLOCKS, tm, V7X_LANES))],
        compiler_params=_params(2),
        name="conformer_conv",
    )(h, *args)


DT_WIDTH = V7X_LANES
TAIL_GROUPS = SSM_CONV - 1


def _split_pack(v, lane):
    hi = v.astype(BF16)
    r1 = v - hi.astype(F32)
    mid = r1.astype(BF16)
    lo = (r1 - mid.astype(F32)).astype(BF16)
    zero = jnp.zeros_like(hi)
    return jnp.where(lane < N_HEADS, hi,
                     jnp.where(lane < 2 * N_HEADS, mid,
                               jnp.where(lane < 3 * N_HEADS, lo, zero)))


def _chunk_cumsum(a_q, sub):
    rows = [a_q[v * V7X_SUBLANES:(v + 1) * V7X_SUBLANES, :] for v in range(ROW_GROUPS)]
    d = 1
    while d < ROW_GROUPS:
        wrapped = [_shift_down(rows[v], 1, sub) for v in range(ROW_GROUPS - d, ROW_GROUPS)]
        rows = [rows[v] + (rows[v - d] if v >= d else wrapped[v]) for v in range(ROW_GROUPS)]
        d *= 2
    j = 1
    while j < V7X_SUBLANES:
        rows = [r + _shift_down(r, j, sub) for r in rows]
        j *= 2
    return jnp.concatenate(rows, axis=0)


def _ssm_kernel(h_ref, g_ref, wzx_ref, wdt_ref, cw_ref, cb_ref, dtb_ref, alog_ref, dsk_ref,
                gn_ref, wout_ref, e64_ref, o_ref, tail_ref, st_ref, y_ref, *, tm):
    nblk = tm // CHUNK

    @pl.when(pl.program_id(1) == 0)
    def _():
        tail_ref[...] = jnp.zeros(tail_ref.shape, F32)
        st_ref[...] = jnp.zeros(st_ref.shape, F32)

    x = h_ref[...]
    xn = _rms_scale(x, g_ref[...]).astype(BF16)
    zx = _dot(xn, wzx_ref[...])
    z = zx[:, :D_INNER]
    u = zx[:, D_INNER:]

    sub_c = lax.broadcasted_iota(jnp.int32, (V7X_SUBLANES, CONV_DIM), 0)
    prev_tail = [tail_ref[i * V7X_SUBLANES:(i + 1) * V7X_SUBLANES, :] for i in range(TAIL_GROUPS)]
    conv_rows = []
    for c in range(nblk):
        grp = [u[c * CHUNK + v * V7X_SUBLANES:c * CHUNK + (v + 1) * V7X_SUBLANES, :]
               for v in range(ROW_GROUPS)]
        tail = grp[ROW_GROUPS - TAIL_GROUPS:]
        wrapped = [jnp.where(sub_c >= 1, pltpu.roll(tail[i], 1, 0), pltpu.roll(prev_tail[i], 1, 0))
                   for i in range(TAIL_GROUPS)]
        for v in range(ROW_GROUPS):
            acc = cb_ref[...]
            for d in range(SSM_CONV):
                src = grp[v - d] if v >= d else wrapped[v - d + TAIL_GROUPS]
                acc = acc + cw_ref[SSM_CONV - 1 - d:SSM_CONV - d, :] * src
            conv_rows.append(acc)
        prev_tail = tail
    for i in range(TAIL_GROUPS):
        tail_ref[i * V7X_SUBLANES:(i + 1) * V7X_SUBLANES, :] = prev_tail[i]
    xbc = _silu(jnp.concatenate(conv_rows, axis=0))
    xs = xbc[:, :D_INNER]
    bm = xbc[:, D_INNER:D_INNER + N_GROUPS * D_STATE]
    cm = xbc[:, D_INNER + N_GROUPS * D_STATE:]

    dtr = _dot(xn, wdt_ref[...]) + dtb_ref[...]
    dt = jnp.maximum(dtr, 0.0) + jnp.log(1.0 + jnp.exp(-jnp.abs(dtr)))
    a = dt * (-jnp.exp(alog_ref[...]))

    sub = lax.broadcasted_iota(jnp.int32, (V7X_SUBLANES, DT_WIDTH), 0)
    lane = lax.broadcasted_iota(jnp.int32, (CHUNK, V7X_LANES), 1)
    low = lane < HEAD_DIM
    low1 = lax.broadcasted_iota(jnp.int32, (1, V7X_LANES), 1) < HEAD_DIM
    lo64 = lax.broadcasted_iota(jnp.int32, (HALF_CHUNK, V7X_LANES), 1) < HEAD_DIM
    pos_r = lax.broadcasted_iota(jnp.int32, (CHUNK, 2 * CHUNK), 0)
    kcol = lax.broadcasted_iota(jnp.int32, (CHUNK, 2 * CHUNK), 1)
    pos_c = HALF_CHUNK * (kcol // CHUNK) + (kcol % HALF_CHUNK)
    time_r = ROW_GROUPS * (pos_r % V7X_SUBLANES) + pos_r // V7X_SUBLANES
    time_c = ROW_GROUPS * (pos_c % V7X_SUBLANES) + pos_c // V7X_SUBLANES
    causal = time_c <= time_r
    e64 = e64_ref[...]

    for q in range(nblk):
        r0 = q * CHUNK
        acs = _chunk_cumsum(a[r0:r0 + CHUNK, :], sub)
        acs_t = acs.T
        acs_tr = pltpu.roll(acs_t, HALF_CHUNK, 1)
        packed = jnp.concatenate([_split_pack(dt[r0:r0 + CHUNK, :], lane),
                                  _split_pack(acs, lane)], axis=0)
        spread = _dot(packed, e64)
        dt64 = spread[:CHUNK]
        acs64 = spread[CHUNK:]
        last64 = acs64[CHUNK - 1:CHUNK, :]
        xdt = xs[r0:r0 + CHUNK, :] * dt64
        xdec = (xdt * jnp.exp(last64 - acs64)).astype(BF16)
        dec_out = jnp.exp(acs64)
        dec_tot = jnp.exp(last64)

        for g in range(N_GROUPS):
            gs = slice(g * GROUP_WIDTH, (g + 1) * GROUP_WIDTH)
            b_gt = bm[r0:r0 + CHUNK, g * D_STATE:(g + 1) * D_STATE].T
            c_g = cm[r0:r0 + CHUNK, g * D_STATE:(g + 1) * D_STATE].astype(BF16)
            b_gtr = pltpu.roll(b_gt, HALF_CHUNK, 1)
            b_dup = jnp.concatenate([jnp.where(low, b_gt, b_gtr),
                                     jnp.where(low, b_gtr, b_gt)], axis=1).astype(BF16)
            cb = _dot(c_g, b_dup)
            st = st_ref[g]
            y_off = _dot(c_g, st.astype(BF16)) * dec_out[:, gs]
            st_ref[g] = st * dec_tot[:, gs] + _dot(b_gt.astype(BF16), xdec[:, gs])

            for jj in range(PAIRS_PER_GROUP):
                pair = g * PAIRS_PER_GROUP + jj
                h0, h1 = 2 * pair, 2 * pair + 1
                ps = slice(pair * PAIR_WIDTH, (pair + 1) * PAIR_WIDTH)
                col = acs64[:, ps]
                row_a = jnp.where(low1, acs_t[h0:h0 + 1, :], acs_tr[h1:h1 + 1, :])
                row_b = jnp.where(low1, acs_tr[h0:h0 + 1, :], acs_t[h1:h1 + 1, :])
                seg = jnp.concatenate([col - row_a, col - row_b], axis=1)
                m = (cb * jnp.exp(jnp.where(causal, seg, MASKED_LOG))).astype(BF16)
                xp = xdt[:, ps]
                top, bot = xp[:HALF_CHUNK], xp[HALF_CHUNK:]
                rhs = jnp.concatenate([jnp.where(lo64, top, 0.0), jnp.where(lo64, 0.0, top),
                                       jnp.where(lo64, bot, 0.0), jnp.where(lo64, 0.0, bot)],
                                      axis=0).astype(BF16)
                y_ref[r0:r0 + CHUNK, ps] = (
                    _dot(m, rhs) + y_off[:, jj * PAIR_WIDTH:(jj + 1) * PAIR_WIDTH])

    y = y_ref[...] + xs * dsk_ref[...]
    hg = y * _silu(z)
    normed = []
    for g in range(N_GROUPS):
        hs = hg[:, g * GROUP_WIDTH:(g + 1) * GROUP_WIDTH]
        ms = jnp.mean(hs * hs, axis=-1, keepdims=True)
        normed.append(hs * lax.rsqrt(ms + EPS))
    hn = (jnp.concatenate(normed, axis=1) * gn_ref[...]).astype(BF16)
    o_ref[...] = x + _dot(hn, wout_ref[...])


def _expander(width):
    e = np.zeros((DT_WIDTH, N_HEADS * width), np.float32)
    for r in range(HEAD_REPLICAS):
        for h in range(N_HEADS):
            e[r * N_HEADS + h, h * width:(h + 1) * width] = 1.0
    return jnp.asarray(e, BF16)


def _replicate_heads(v):
    rep = jnp.tile(v, HEAD_REPLICAS)
    return jnp.pad(rep, (0, DT_WIDTH - rep.shape[0]))[None, :]


def _ssm_layer(h, g, w_in, conv_w, conv_b, dt_bias, a_log, d_skip, gate_norm, w_out, *, tm):
    b, l, d = h.shape
    n_zx = D_INNER + CONV_DIM
    wzx = w_in[:, :n_zx].astype(BF16)
    wdt = w_in[:, n_zx:]
    wdt = jnp.pad(jnp.tile(wdt, (1, HEAD_REPLICAS)),
                  ((0, 0), (0, DT_WIDTH - HEAD_REPLICAS * N_HEADS))).astype(BF16)
    args = (g[None, :], wzx, wdt, conv_w, conv_b[None, :], _replicate_heads(dt_bias),
            _replicate_heads(a_log), jnp.repeat(d_skip, HEAD_DIM)[None, :], gate_norm[None, :],
            w_out.astype(BF16), _expander(HEAD_DIM))
    tile = pl.BlockSpec((None, tm, d), lambda i, j: (i, j, 0))
    return pl.pallas_call(
        functools.partial(_ssm_kernel, tm=tm),
        grid=(b, l // tm),
        in_specs=[tile] + [_resident(a.shape) for a in args],
        out_specs=tile,
        out_shape=jax.ShapeDtypeStruct(h.shape, h.dtype),
        scratch_shapes=[pltpu.VMEM((TAIL_GROUPS * V7X_SUBLANES, CONV_DIM), F32),
                        pltpu.VMEM((N_GROUPS, D_STATE, GROUP_WIDTH), F32),
                        pltpu.VMEM((tm, D_INNER), F32)],
        compiler_params=_params(2),
        name="ssd_mixer",
    )(h, *args)


SSM_TILE = 256
CONF_TILE = 512
FFN_TILE = 512


def kernel(x, ssm_norm, ssm_w_in, ssm_conv_w, ssm_conv_b, ssm_dt_bias, ssm_a_log, ssm_d, ssm_gate_norm, ssm_w_out, cv_norm, cv_w_pw1, cv_b_pw1, cv_dw_w, cv_dw_b, cv_ln_g, cv_ln_b, cv_w_pw2, cv_b_pw2, ffn_norm, ffn_w_gate, ffn_w_up, ffn_w_down, final_norm):
    h = _interleave(x)
    for i in range(DEPTH):
        j = i // 2
        if i % 2 == 0:
            h = _ssm_layer(h, ssm_norm[j], ssm_w_in[j], ssm_conv_w[j], ssm_conv_b[j],
                           ssm_dt_bias[j], ssm_a_log[j], ssm_d[j], ssm_gate_norm[j],
                           ssm_w_out[j], tm=SSM_TILE)
        else:
            h = _conf_layer(h, cv_norm[j][None, :], cv_w_pw1[j].astype(BF16),
                            cv_b_pw1[j][None, :], cv_dw_w[j], cv_dw_b[j][None, :],
                            cv_ln_g[j][None, :], cv_ln_b[j][None, :],
                            cv_w_pw2[j].astype(BF16), cv_b_pw2[j][None, :], tm=CONF_TILE)
        h = _ffn_layer(h, ffn_norm[i][None, :], ffn_w_gate[i].astype(BF16),
                       ffn_w_up[i].astype(BF16), ffn_w_down[i].astype(BF16),
                       final_norm[None, :], final_norm=(i == DEPTH - 1), tm=FFN_TILE)
    return _deinterleave(h)
```

```python
import functools

import numpy as np
import jax
import jax.numpy as jnp
from jax import lax
from jax.experimental import pallas as pl
from jax.experimental.pallas import tpu as pltpu

D_MODEL = 1024
DEPTH = 4
D_INNER = 2048
HEAD_DIM = 64
N_HEADS = 32
N_GROUPS = 4
HEADS_PER_GROUP = 8
D_STATE = 128
SSM_CONV = 4
CONV_DIM = D_INNER + 2 * N_GROUPS * D_STATE
CHUNK = 128
CONV_KERNEL = 31
D_FF = 2816
EPS = 1e-5

V7X_LANES = 128
V7X_SUBLANES = 8
V7X_VMEM_LIMIT_BYTES = 60 * 1024 * 1024

ROW_GROUPS = CHUNK // V7X_SUBLANES
GROUP_WIDTH = HEADS_PER_GROUP * HEAD_DIM
HEAD_REPLICAS = 3
PAIR_WIDTH = 2 * HEAD_DIM
PAIRS_PER_GROUP = HEADS_PER_GROUP // 2
HALF_CHUNK = CHUNK // 2
MASKED_LOG = -1e30
LANE_BLOCKS = D_MODEL // V7X_LANES

F32 = jnp.float32
BF16 = jnp.bfloat16


def _sigmoid(x):
    return 1.0 / (1.0 + jnp.exp(-x))


def _silu(x):
    return x * _sigmoid(x)


def _rms_scale(x, g):
    ms = jnp.mean(x * x, axis=-1, keepdims=True)
    return x * lax.rsqrt(ms + EPS) * g


def _dot(a, b):
    return jnp.dot(a, b, preferred_element_type=F32)


def _resident(shape):
    nd = len(shape)
    return pl.BlockSpec(shape, lambda *_: (0,) * nd, pipeline_mode=pl.Buffered(1))


def _params(n_grid):
    return pltpu.CompilerParams(
        dimension_semantics=("arbitrary",) * n_grid,
        vmem_limit_bytes=V7X_VMEM_LIMIT_BYTES)


def _interleave(x):
    b, l, d = x.shape
    x = x.reshape(b, l // CHUNK, V7X_SUBLANES, ROW_GROUPS, d)
    return x.swapaxes(2, 3).reshape(b, l, d)


def _deinterleave(x):
    b, l, d = x.shape
    x = x.reshape(b, l // CHUNK, ROW_GROUPS, V7X_SUBLANES, d)
    return x.swapaxes(2, 3).reshape(b, l, d)


def _shift_down(x, j, sub):
    return jnp.where(sub >= j, pltpu.roll(x, j, 0), 0.0)


def _ffn_kernel(h_ref, g_ref, wg_ref, wu_ref, wd_ref, fg_ref, o_ref, *, final_norm):
    x = h_ref[...]
    xn = _rms_scale(x, g_ref[...]).astype(BF16)
    gate = _dot(xn, wg_ref[...])
    up = _dot(xn, wu_ref[...])
    act = (_silu(gate) * up).astype(BF16)
    y = x + _dot(act, wd_ref[...])
    if final_norm:
        y = _rms_scale(y, fg_ref[...])
    o_ref[...] = y


def _ffn_layer(h, g, wg, wu, wd, fg, *, final_norm, tm):
    b, l, d = h.shape
    tile = pl.BlockSpec((None, tm, d), lambda i, j: (i, j, 0))
    return pl.pallas_call(
        functools.partial(_ffn_kernel, final_norm=final_norm),
        grid=(b, l // tm),
        in_specs=[tile, _resident(g.shape), _resident(wg.shape), _resident(wu.shape),
                  _resident(wd.shape), _resident(fg.shape)],
        out_specs=tile,
        out_shape=jax.ShapeDtypeStruct(h.shape, h.dtype),
        compiler_params=_params(2),
        name="swiglu_ffn",
    )(h, g, wg, wu, wd, fg)


FFN_CHUNK = 256
FFN_CHUNKS = D_FF // FFN_CHUNK


def _conf_ffn_kernel(h_ref, g_ref, w1_ref, b1_ref, dww_ref, dwb_ref, lng_ref, lnb_ref,
                     w2_ref, b2_ref, fg_ref, wg_ref, wu_ref, wd_ref, ng_ref, o_ref,
                     u0_ref, u1_ref, u2_ref, acc_ref, hc_ref, facc_ref, act_ref,
                     *, tm, tiles_per_seq, final_norm):
    nblk = tm // CHUNK
    n_units = LANE_BLOCKS * nblk
    units_per_piece = n_units // (FFN_CHUNKS - 1)
    step = pl.program_id(0)

    @pl.when(step == 0)
    def _():
        hc_ref[...] = jnp.zeros(hc_ref.shape, F32)

    @pl.when(step % tiles_per_seq == 0)
    def _():
        u0_ref[:, 0:CHUNK, :] = jnp.zeros((LANE_BLOCKS, CHUNK, V7X_LANES), F32)

    xn = _rms_scale(h_ref[...], g_ref[...]).astype(BF16)
    hh = _dot(xn, w1_ref[...]) + b1_ref[...]
    u = hh[:, :D_MODEL] * _sigmoid(hh[:, D_MODEL:])
    for lb in range(LANE_BLOCKS):
        u0_ref[lb, CHUNK:CHUNK + tm, :] = u[:, lb * V7X_LANES:(lb + 1) * V7X_LANES]

    xf = _rms_scale(hc_ref[...], fg_ref[...]).astype(BF16)
    facc_ref[...] = jnp.zeros(facc_ref.shape, F32)
    sub = lax.broadcasted_iota(jnp.int32, (V7X_SUBLANES, V7X_LANES), 0)

    def conv_unit(unit):
        lb = unit // nblk
        base = (unit % nblk) * CHUNK
        if not isinstance(unit, int):
            base = pl.multiple_of(base, CHUNK)

        def group(ref, off, v):
            return ref[lb, pl.ds(base + off + v * V7X_SUBLANES, V7X_SUBLANES), :]

        for v in range(ROW_GROUPS):
            cur = group(u0_ref, CHUNK, v)
            prv = group(u0_ref, 0, v)
            dst = pl.ds(base + v * V7X_SUBLANES, V7X_SUBLANES)
            u1_ref[lb, dst, :] = jnp.where(sub >= 1, pltpu.roll(cur, 1, 0), pltpu.roll(prv, 1, 0))
            u2_ref[lb, dst, :] = jnp.where(sub >= 2, pltpu.roll(cur, 2, 0), pltpu.roll(prv, 2, 0))

        w = [jnp.broadcast_to(dww_ref[lb, k:k + 1, :], (V7X_SUBLANES, V7X_LANES))
             for k in range(CONV_KERNEL)]
        bias = jnp.broadcast_to(dwb_ref[lb], (V7X_SUBLANES, V7X_LANES))
        for v in range(ROW_GROUPS):
            acc = bias
            for d in range(CONV_KERNEL):
                a, r = divmod(d, ROW_GROUPS)
                vv, j = (v - r, a) if v >= r else (v - r + ROW_GROUPS, a + 1)
                if j == 0:
                    src = group(u0_ref, CHUNK, vv)
                elif j == 1:
                    src = group(u1_ref, 0, vv)
                else:
                    src = group(u2_ref, 0, vv)
                acc = acc + w[CONV_KERNEL - 1 - d] * src
            acc_ref[lb, pl.ds(base + v * V7X_SUBLANES, V7X_SUBLANES), :] = acc

    def ffn_act(k):
        return (_silu(_dot(xf, wg_ref[k])) * _dot(xf, wu_ref[k])).astype(BF16)

    def piece(k, carry):
        nxt = ffn_act(k + 1)
        facc_ref[...] += _dot(act_ref[...], wd_ref[k])
        act_ref[...] = nxt
        for t in range(units_per_piece):
            conv_unit(k * units_per_piece + t)
        return carry

    act_ref[...] = ffn_act(0)
    lax.fori_loop(0, FFN_CHUNKS - 1, piece, 0)
    for unit in range((FFN_CHUNKS - 1) * units_per_piece, n_units):
        conv_unit(unit)

    y = hc_ref[...] + facc_ref[...] + _dot(act_ref[...], wd_ref[FFN_CHUNKS - 1])
    if final_norm:
        y = _rms_scale(y, ng_ref[...])
    o_ref[...] = y

    conv = jnp.concatenate([acc_ref[lb] for lb in range(LANE_BLOCKS)], axis=1)
    mu = jnp.mean(conv, axis=-1, keepdims=True)
    xc = conv - mu
    var = jnp.mean(xc * xc, axis=-1, keepdims=True)
    s = _silu(xc * lax.rsqrt(var + EPS) * lng_ref[...] + lnb_ref[...]).astype(BF16)
    hc_ref[...] = h_ref[...] + _dot(s, w2_ref[...]) + b2_ref[...]
    u0_ref[:, 0:CHUNK, :] = u0_ref[:, tm:tm + CHUNK, :]


def _lane_blocked(w):
    return w.reshape(w.shape[0], LANE_BLOCKS, V7X_LANES).swapaxes(0, 1)


def _conf_ffn_layer(h, g, w1, b1, dww, dwb, lng, lnb, w2, b2, fg, wg, wu, wd, ng,
                    *, tm, final_norm):
    b, l, d = h.shape
    tiles_per_seq = l // tm
    n_tiles = b * tiles_per_seq

    def tile_of(t):
        return (t // tiles_per_seq, t % tiles_per_seq, 0)

    wg = wg.reshape(d, FFN_CHUNKS, FFN_CHUNK).swapaxes(0, 1)
    wu = wu.reshape(d, FFN_CHUNKS, FFN_CHUNK).swapaxes(0, 1)
    wd = wd.reshape(FFN_CHUNKS, FFN_CHUNK, d)
    args = (g, w1, b1, _lane_blocked(dww), _lane_blocked(dwb), lng, lnb, w2, b2, fg, wg, wu, wd, ng)
    blocked = functools.partial(pltpu.VMEM, dtype=F32)
    return pl.pallas_call(
        functools.partial(_conf_ffn_kernel, tm=tm, tiles_per_seq=tiles_per_seq,
                          final_norm=final_norm),
        grid=(n_tiles + 1,),
        in_specs=[pl.BlockSpec((None, tm, d), lambda i: tile_of(jnp.minimum(i, n_tiles - 1)))]
        + [_resident(a.shape) for a in args],
        out_specs=pl.BlockSpec((None, tm, d), lambda i: tile_of(jnp.maximum(i - 1, 0))),
        out_shape=jax.ShapeDtypeStruct(h.shape, h.dtype),
        scratch_shapes=[blocked((LANE_BLOCKS, CHUNK + tm, V7X_LANES)),
                        blocked((LANE_BLOCKS, tm, V7X_LANES)),
                        blocked((LANE_BLOCKS, tm, V7X_LANES)),
                        blocked((LANE_BLOCKS, tm, V7X_LANES)),
                        blocked((tm, d)),
                        blocked((tm, d)),
                        pltpu.VMEM((tm, FFN_CHUNK), BF16)],
        compiler_params=_params(1),
        name="conformer_conv_ffn",
    )(h, *args)


DT_WIDTH = V7X_LANES
TAIL_GROUPS = SSM_CONV - 1


def _split_pack(v, lane):
    hi = v.astype(BF16)
    r1 = v - hi.astype(F32)
    mid = r1.astype(BF16)
    lo = (r1 - mid.astype(F32)).astype(BF16)
    zero = jnp.zeros_like(hi)
    return jnp.where(lane < N_HEADS, hi,
                     jnp.where(lane < 2 * N_HEADS, mid,
                               jnp.where(lane < 3 * N_HEADS, lo, zero)))


def _chunk_cumsum(a_q, sub):
    rows = [a_q[v * V7X_SUBLANES:(v + 1) * V7X_SUBLANES, :] for v in range(ROW_GROUPS)]
    d = 1
    while d < ROW_GROUPS:
        wrapped = [_shift_down(rows[v], 1, sub) for v in range(ROW_GROUPS - d, ROW_GROUPS)]
        rows = [rows[v] + (rows[v - d] if v >= d else wrapped[v]) for v in range(ROW_GROUPS)]
        d *= 2
    j = 1
    while j < V7X_SUBLANES:
        rows = [r + _shift_down(r, j, sub) for r in rows]
        j *= 2
    return jnp.concatenate(rows, axis=0)


def _ssm_ffn_kernel(h_ref, g_ref, wzx_ref, wdt_ref, cw_ref, cb_ref, dtb_ref, alog_ref, dsk_ref,
                    gn_ref, wout_ref, e64_ref, fg_ref, wg_ref, wu_ref, wd_ref, o_ref,
                    tail_ref, st_ref, y_ref, hc_ref, *, tm, tiles_per_seq):
    nblk = tm // CHUNK
    step = pl.program_id(0)

    @pl.when(step == 0)
    def _():
        hc_ref[...] = jnp.zeros(hc_ref.shape, F32)

    @pl.when(step % tiles_per_seq == 0)
    def _():
        tail_ref[...] = jnp.zeros(tail_ref.shape, F32)
        st_ref[...] = jnp.zeros(st_ref.shape, F32)

    xf = _rms_scale(hc_ref[...], fg_ref[...]).astype(BF16)
    ffn_out = [None]

    def ffn_pieces():
        for k in range(FFN_CHUNKS):
            cs = slice(k * FFN_CHUNK, (k + 1) * FFN_CHUNK)
            act = (_silu(_dot(xf, wg_ref[:, cs])) * _dot(xf, wu_ref[:, cs])).astype(BF16)
            part = _dot(act, wd_ref[cs, :])
            ffn_out[0] = part if ffn_out[0] is None else ffn_out[0] + part
            yield

    ffn = ffn_pieces()

    x = h_ref[...]
    xn = _rms_scale(x, g_ref[...]).astype(BF16)
    zx = _dot(xn, wzx_ref[...])
    z = zx[:, :D_INNER]
    u = zx[:, D_INNER:]

    sub_c = lax.broadcasted_iota(jnp.int32, (V7X_SUBLANES, CONV_DIM), 0)
    prev_tail = [tail_ref[i * V7X_SUBLANES:(i + 1) * V7X_SUBLANES, :] for i in range(TAIL_GROUPS)]
    conv_rows = []
    for c in range(nblk):
        grp = [u[c * CHUNK + v * V7X_SUBLANES:c * CHUNK + (v + 1) * V7X_SUBLANES, :]
               for v in range(ROW_GROUPS)]
        tail = grp[ROW_GROUPS - TAIL_GROUPS:]
        wrapped = [jnp.where(sub_c >= 1, pltpu.roll(tail[i], 1, 0), pltpu.roll(prev_tail[i], 1, 0))
                   for i in range(TAIL_GROUPS)]
        for v in range(ROW_GROUPS):
            acc = cb_ref[...]
            for d in range(SSM_CONV):
                src = grp[v - d] if v >= d else wrapped[v - d + TAIL_GROUPS]
                acc = acc + cw_ref[SSM_CONV - 1 - d:SSM_CONV - d, :] * src
            conv_rows.append(acc)
        prev_tail = tail
    for i in range(TAIL_GROUPS):
        tail_ref[i * V7X_SUBLANES:(i + 1) * V7X_SUBLANES, :] = prev_tail[i]
    xbc = _silu(jnp.concatenate(conv_rows, axis=0))
    xs = xbc[:, :D_INNER]
    bm = xbc[:, D_INNER:D_INNER + N_GROUPS * D_STATE]
    cm = xbc[:, D_INNER + N_GROUPS * D_STATE:]

    dtr = _dot(xn, wdt_ref[...]) + dtb_ref[...]
    dt = jnp.maximum(dtr, 0.0) + jnp.log(1.0 + jnp.exp(-jnp.abs(dtr)))
    a = dt * (-jnp.exp(alog_ref[...]))

    sub = lax.broadcasted_iota(jnp.int32, (V7X_SUBLANES, DT_WIDTH), 0)
    lane = lax.broadcasted_iota(jnp.int32, (CHUNK, V7X_LANES), 1)
    low = lane < HEAD_DIM
    low1 = lax.broadcasted_iota(jnp.int32, (1, V7X_LANES), 1) < HEAD_DIM
    lo64 = lax.broadcasted_iota(jnp.int32, (HALF_CHUNK, V7X_LANES), 1) < HEAD_DIM
    pos_r = lax.broadcasted_iota(jnp.int32, (CHUNK, 2 * CHUNK), 0)
    kcol = lax.broadcasted_iota(jnp.int32, (CHUNK, 2 * CHUNK), 1)
    pos_c = HALF_CHUNK * (kcol // CHUNK) + (kcol % HALF_CHUNK)
    time_r = ROW_GROUPS * (pos_r % V7X_SUBLANES) + pos_r // V7X_SUBLANES
    time_c = ROW_GROUPS * (pos_c % V7X_SUBLANES) + pos_c // V7X_SUBLANES
    causal = time_c <= time_r
    e64 = e64_ref[...]

    for q in range(nblk):
        r0 = q * CHUNK
        acs = _chunk_cumsum(a[r0:r0 + CHUNK, :], sub)
        acs_t = acs.T
        acs_tr = pltpu.roll(acs_t, HALF_CHUNK, 1)
        packed = jnp.concatenate([_split_pack(dt[r0:r0 + CHUNK, :], lane),
                                  _split_pack(acs, lane)], axis=0)
        spread = _dot(packed, e64)
        dt64 = spread[:CHUNK]
        acs64 = spread[CHUNK:]
        last64 = acs64[CHUNK - 1:CHUNK, :]
        xdt = xs[r0:r0 + CHUNK, :] * dt64
        xdec = (xdt * jnp.exp(last64 - acs64)).astype(BF16)
        dec_out = jnp.exp(acs64)
        dec_tot = jnp.exp(last64)

        for g in range(N_GROUPS):
            next(ffn, None)
            gs = slice(g * GROUP_WIDTH, (g + 1) * GROUP_WIDTH)
            b_gt = bm[r0:r0 + CHUNK, g * D_STATE:(g + 1) * D_STATE].T
            c_g = cm[r0:r0 + CHUNK, g * D_STATE:(g + 1) * D_STATE].astype(BF16)
            b_gtr = pltpu.roll(b_gt, HALF_CHUNK, 1)
            b_dup = jnp.concatenate([jnp.where(low, b_gt, b_gtr),
                                     jnp.where(low, b_gtr, b_gt)], axis=1).astype(BF16)
            cb = _dot(c_g, b_dup)
            st = st_ref[g]
            y_off = _dot(c_g, st.astype(BF16)) * dec_out[:, gs]
            st_ref[g] = st * dec_tot[:, gs] + _dot(b_gt.astype(BF16), xdec[:, gs])

            for jj in range(PAIRS_PER_GROUP):
                pair = g * PAIRS_PER_GROUP + jj
                h0, h1 = 2 * pair, 2 * pair + 1
                ps = slice(pair * PAIR_WIDTH, (pair + 1) * PAIR_WIDTH)
                col = acs64[:, ps]
                row_a = jnp.where(low1, acs_t[h0:h0 + 1, :], acs_tr[h1:h1 + 1, :])
                row_b = jnp.where(low1, acs_tr[h0:h0 + 1, :], acs_t[h1:h1 + 1, :])
                seg = jnp.concatenate([col - row_a, col - row_b], axis=1)
                m = (cb * jnp.exp(jnp.where(causal, seg, MASKED_LOG))).astype(BF16)
                xp = xdt[:, ps]
                top, bot = xp[:HALF_CHUNK], xp[HALF_CHUNK:]
                rhs = jnp.concatenate([jnp.where(lo64, top, 0.0), jnp.where(lo64, 0.0, top),
                                       jnp.where(lo64, bot, 0.0), jnp.where(lo64, 0.0, bot)],
                                      axis=0).astype(BF16)
                y_ref[r0:r0 + CHUNK, ps] = (
                    _dot(m, rhs) + y_off[:, jj * PAIR_WIDTH:(jj + 1) * PAIR_WIDTH])

    for _ in ffn:
        pass
    o_ref[...] = hc_ref[...] + ffn_out[0]

    y = y_ref[...] + xs * dsk_ref[...]
    hg = y * _silu(z)
    normed = []
    for g in range(N_GROUPS):
        hs = hg[:, g * GROUP_WIDTH:(g + 1) * GROUP_WIDTH]
        ms = jnp.mean(hs * hs, axis=-1, keepdims=True)
        normed.append(hs * lax.rsqrt(ms + EPS))
    hn = (jnp.concatenate(normed, axis=1) * gn_ref[...]).astype(BF16)
    hc_ref[...] = x + _dot(hn, wout_ref[...])


def _expander(width):
    e = np.zeros((DT_WIDTH, N_HEADS * width), np.float32)
    for r in range(HEAD_REPLICAS):
        for h in range(N_HEADS):
            e[r * N_HEADS + h, h * width:(h + 1) * width] = 1.0
    return jnp.asarray(e, BF16)


def _replicate_heads(v):
    rep = jnp.tile(v, HEAD_REPLICAS)
    return jnp.pad(rep, (0, DT_WIDTH - rep.shape[0]))[None, :]


def _ssm_ffn_layer(h, g, w_in, conv_w, conv_b, dt_bias, a_log, d_skip, gate_norm, w_out,
                   fg, wg, wu, wd, *, tm):
    b, l, d = h.shape
    tiles_per_seq = l // tm
    n_tiles = b * tiles_per_seq

    def tile_of(t):
        return (t // tiles_per_seq, t % tiles_per_seq, 0)

    n_zx = D_INNER + CONV_DIM
    wzx = w_in[:, :n_zx].astype(BF16)
    wdt = w_in[:, n_zx:]
    wdt = jnp.pad(jnp.tile(wdt, (1, HEAD_REPLICAS)),
                  ((0, 0), (0, DT_WIDTH - HEAD_REPLICAS * N_HEADS))).astype(BF16)
    args = (g[None, :], wzx, wdt, conv_w, conv_b[None, :], _replicate_heads(dt_bias),
            _replicate_heads(a_log), jnp.repeat(d_skip, HEAD_DIM)[None, :], gate_norm[None, :],
            w_out.astype(BF16), _expander(HEAD_DIM), fg, wg, wu, wd)
    return pl.pallas_call(
        functools.partial(_ssm_ffn_kernel, tm=tm, tiles_per_seq=tiles_per_seq),
        grid=(n_tiles + 1,),
        in_specs=[pl.BlockSpec((None, tm, d), lambda i: tile_of(jnp.minimum(i, n_tiles - 1)))]
        + [_resident(a.shape) for a in args],
        out_specs=pl.BlockSpec((None, tm, d), lambda i: tile_of(jnp.maximum(i - 1, 0))),
        out_shape=jax.ShapeDtypeStruct(h.shape, h.dtype),
        scratch_shapes=[pltpu.VMEM((TAIL_GROUPS * V7X_SUBLANES, CONV_DIM), F32),
                        pltpu.VMEM((N_GROUPS, D_STATE, GROUP_WIDTH), F32),
                        pltpu.VMEM((tm, D_INNER), F32),
                        pltpu.VMEM((tm, d), F32)],
        compiler_params=_params(1),
        name="ssd_mixer_ffn",
    )(h, *args)


SSM_TILE = 256
CONF_TILE = 512
FFN_TILE = 512


def kernel(x, ssm_norm, ssm_w_in, ssm_conv_w, ssm_conv_b, ssm_dt_bias, ssm_a_log, ssm_d, ssm_gate_norm, ssm_w_out, cv_norm, cv_w_pw1, cv_b_pw1, cv_dw_w, cv_dw_b, cv_ln_g, cv_ln_b, cv_w_pw2, cv_b_pw2, ffn_norm, ffn_w_gate, ffn_w_up, ffn_w_down, final_norm):
    h = _interleave(x)
    for i in range(DEPTH):
        j = i // 2
        ffn = (ffn_norm[i][None, :], ffn_w_gate[i].astype(BF16), ffn_w_up[i].astype(BF16),
               ffn_w_down[i].astype(BF16), final_norm[None, :])
        if i % 2 == 0:
            h = _ssm_ffn_layer(h, ssm_norm[j], ssm_w_in[j], ssm_conv_w[j], ssm_conv_b[j],
                               ssm_dt_bias[j], ssm_a_log[j], ssm_d[j], ssm_gate_norm[j],
                               ssm_w_out[j], *ffn[:4], tm=SSM_TILE)
        else:
            h = _conf_ffn_layer(h, cv_norm[j][None, :], cv_w_pw1[j].astype(BF16),
                                cv_b_pw1[j][None, :], cv_dw_w[j], cv_dw_b[j][None, :],
                                cv_ln_g[j][None, :], cv_ln_b[j][None, :],
                                cv_w_pw2[j].astype(BF16), cv_b_pw2[j][None, :], *ffn,
                                tm=CONF_TILE, final_norm=(i == DEPTH - 1))
    return _deinterleave(h)
```

```python
import functools

import numpy as np
import jax
import jax.numpy as jnp
from jax import lax
from jax.experimental import pallas as pl
from jax.experimental.pallas import tpu as pltpu

D_MODEL = 1024
DEPTH = 4
D_INNER = 2048
HEAD_DIM = 64
N_HEADS = 32
N_GROUPS = 4
HEADS_PER_GROUP = 8
D_STATE = 128
SSM_CONV = 4
CONV_DIM = D_INNER + 2 * N_GROUPS * D_STATE
CHUNK = 128
CONV_KERNEL = 31
D_FF = 2816
EPS = 1e-5

V7X_LANES = 128
V7X_SUBLANES = 8
V7X_VMEM_LIMIT_BYTES = 56 * 1024 * 1024

ROW_GROUPS = CHUNK // V7X_SUBLANES
GROUP_WIDTH = HEADS_PER_GROUP * HEAD_DIM
HEAD_REPLICAS = 3
PAIR_WIDTH = 2 * HEAD_DIM
PAIRS_PER_GROUP = HEADS_PER_GROUP // 2
HALF_CHUNK = CHUNK // 2
MASKED_LOG = -1e30
LANE_BLOCKS = D_MODEL // V7X_LANES
CAST_ROWS = 256

F32 = jnp.float32
BF16 = jnp.bfloat16


def _sigmoid(x):
    return 1.0 / (1.0 + jnp.exp(-x))


def _silu(x):
    return x * _sigmoid(x)


def _rms_scale(x, g):
    ms = jnp.mean(x * x, axis=-1, keepdims=True)
    return x * lax.rsqrt(ms + EPS) * g


def _dot(a, b):
    return jnp.dot(a, b, preferred_element_type=F32)


def _resident(shape):
    nd = len(shape)
    return pl.BlockSpec(shape, lambda *_: (0,) * nd, pipeline_mode=pl.Buffered(1))


def _layer_of(stacked, layer, shape=None):
    shape = tuple(stacked.shape[1:]) if shape is None else shape
    nd = len(shape)
    return pl.BlockSpec((None,) + shape, lambda *_: (layer,) + (0,) * nd,
                        pipeline_mode=pl.Buffered(1))


def _cast_kernel(x_ref, o_ref):
    o_ref[...] = x_ref[...].astype(o_ref.dtype)


def _to_bf16(w):
    n, r, c = w.shape
    spec = pl.BlockSpec((None, CAST_ROWS, c), lambda i, j: (i, j, 0))
    return pl.pallas_call(
        _cast_kernel,
        grid=(n, r // CAST_ROWS),
        in_specs=[spec],
        out_specs=spec,
        out_shape=jax.ShapeDtypeStruct(w.shape, BF16),
        compiler_params=_params(2),
        name="cast_bf16",
    )(w)


def _params(n_grid):
    return pltpu.CompilerParams(
        dimension_semantics=("arbitrary",) * n_grid,
        vmem_limit_bytes=V7X_VMEM_LIMIT_BYTES)


def _interleave(x):
    b, l, d = x.shape
    x = x.reshape(b, l // CHUNK, V7X_SUBLANES, ROW_GROUPS, d)
    return x.swapaxes(2, 3).reshape(b, l, d)


def _deinterleave(x):
    b, l, d = x.shape
    x = x.reshape(b, l // CHUNK, ROW_GROUPS, V7X_SUBLANES, d)
    return x.swapaxes(2, 3).reshape(b, l, d)


def _shift_down(x, j, sub):
    return jnp.where(sub >= j, pltpu.roll(x, j, 0), 0.0)


def _ffn_kernel(h_ref, g_ref, wg_ref, wu_ref, wd_ref, fg_ref, o_ref, *, final_norm):
    x = h_ref[...]
    xn = _rms_scale(x, g_ref[...]).astype(BF16)
    gate = _dot(xn, wg_ref[...])
    up = _dot(xn, wu_ref[...])
    act = (_silu(gate) * up).astype(BF16)
    y = x + _dot(act, wd_ref[...])
    if final_norm:
        y = _rms_scale(y, fg_ref[...])
    o_ref[...] = y


def _ffn_layer(h, layer, g, wg, wu, wd, fg, *, final_norm, tm):
    b, l, d = h.shape
    tile = pl.BlockSpec((None, tm, d), lambda i, j: (i, j, 0))
    return pl.pallas_call(
        functools.partial(_ffn_kernel, final_norm=final_norm),
        grid=(b, l // tm),
        in_specs=[tile, _resident(g.shape), _layer_of(wg, layer), _layer_of(wu, layer),
                  _layer_of(wd, layer), _resident(fg.shape)],
        out_specs=tile,
        out_shape=jax.ShapeDtypeStruct(h.shape, h.dtype),
        compiler_params=_params(2),
        name="swiglu_ffn",
    )(h, g, wg, wu, wd, fg)


def _conf_kernel(h_ref, g_ref, w1_ref, b1_ref, dww_ref, dwb_ref, lng_ref, lnb_ref,
                 w2_ref, b2_ref, o_ref, u0_ref, u1_ref, u2_ref, acc_ref, *, tm):
    nblk = tm // CHUNK

    @pl.when(pl.program_id(1) == 0)
    def _():
        u0_ref[:, 0:CHUNK, :] = jnp.zeros((LANE_BLOCKS, CHUNK, V7X_LANES), F32)

    x = h_ref[...]
    xn = _rms_scale(x, g_ref[...]).astype(BF16)
    hh = _dot(xn, w1_ref[...]) + b1_ref[...]
    u = hh[:, :D_MODEL] * _sigmoid(hh[:, D_MODEL:])
    for lb in range(LANE_BLOCKS):
        u0_ref[lb, CHUNK:CHUNK + tm, :] = u[:, lb * V7X_LANES:(lb + 1) * V7X_LANES]

    sub = lax.broadcasted_iota(jnp.int32, (V7X_SUBLANES, V7X_LANES), 0)

    def lane_block(lb, carry):
        def group(ref, c, v):
            r = c * CHUNK + v * V7X_SUBLANES
            return ref[lb, r:r + V7X_SUBLANES, :]

        prev1 = [pltpu.roll(group(u0_ref, 0, v), 1, 0) for v in range(ROW_GROUPS)]
        prev2 = [pltpu.roll(group(u0_ref, 0, v), 2, 0) for v in range(ROW_GROUPS)]
        for c in range(nblk):
            for v in range(ROW_GROUPS):
                cur = group(u0_ref, c + 1, v)
                r1 = pltpu.roll(cur, 1, 0)
                r2 = pltpu.roll(cur, 2, 0)
                r = c * CHUNK + v * V7X_SUBLANES
                u1_ref[lb, r:r + V7X_SUBLANES, :] = jnp.where(sub >= 1, r1, prev1[v])
                u2_ref[lb, r:r + V7X_SUBLANES, :] = jnp.where(sub >= 2, r2, prev2[v])
                prev1[v] = r1
                prev2[v] = r2

        w = [jnp.broadcast_to(dww_ref[lb, k:k + 1, :], (V7X_SUBLANES, V7X_LANES))
             for k in range(CONV_KERNEL)]
        bias = jnp.broadcast_to(dwb_ref[lb], (V7X_SUBLANES, V7X_LANES))
        for c in range(nblk):
            for v in range(ROW_GROUPS):
                acc = bias
                for d in range(CONV_KERNEL):
                    a, r = divmod(d, ROW_GROUPS)
                    vv, j = (v - r, a) if v >= r else (v - r + ROW_GROUPS, a + 1)
                    if j == 0:
                        src = group(u0_ref, c + 1, vv)
                    elif j == 1:
                        src = group(u1_ref, c, vv)
                    else:
                        src = group(u2_ref, c, vv)
                    acc = acc + w[CONV_KERNEL - 1 - d] * src
                r = c * CHUNK + v * V7X_SUBLANES
                acc_ref[lb, r:r + V7X_SUBLANES, :] = acc
        return carry

    lax.fori_loop(0, LANE_BLOCKS, lane_block, 0)
    u0_ref[:, 0:CHUNK, :] = u0_ref[:, tm:tm + CHUNK, :]

    conv = jnp.concatenate([acc_ref[lb] for lb in range(LANE_BLOCKS)], axis=1)
    mu = jnp.mean(conv, axis=-1, keepdims=True)
    xc = conv - mu
    var = jnp.mean(xc * xc, axis=-1, keepdims=True)
    y = xc * lax.rsqrt(var + EPS) * lng_ref[...] + lnb_ref[...]
    s = _silu(y).astype(BF16)
    o_ref[...] = x + _dot(s, w2_ref[...]) + b2_ref[...]


def _lane_blocked(w):
    return w.reshape(w.shape[0], LANE_BLOCKS, V7X_LANES).swapaxes(0, 1)


def _conf_layer(h, layer, g, w1, b1, dww, dwb, lng, lnb, w2, b2, *, tm):
    b, l, d = h.shape
    tile = pl.BlockSpec((None, tm, d), lambda i, j: (i, j, 0))
    args = (g, w1, b1, _lane_blocked(dww), _lane_blocked(dwb), lng, lnb, w2, b2)
    specs = [_layer_of(a, layer) if a is w1 or a is w2 else _resident(a.shape) for a in args]
    blocked = functools.partial(pltpu.VMEM, dtype=F32)
    return pl.pallas_call(
        functools.partial(_conf_kernel, tm=tm),
        grid=(b, l // tm),
        in_specs=[tile] + specs,
        out_specs=tile,
        out_shape=jax.ShapeDtypeStruct(h.shape, h.dtype),
        scratch_shapes=[blocked((LANE_BLOCKS, CHUNK + tm, V7X_LANES)),
                        blocked((LANE_BLOCKS, tm, V7X_LANES)),
                        blocked((LANE_BLOCKS, tm, V7X_LANES)),
                        blocked((LANE_BLOCKS, tm, V7X_LANES))],
        compiler_params=_params(2),
        name="conformer_conv",
    )(h, *args)


DT_WIDTH = V7X_LANES
TAIL_GROUPS = SSM_CONV - 1


def _split_pack(v, lane):
    hi = v.astype(BF16)
    r1 = v - hi.astype(F32)
    mid = r1.astype(BF16)
    lo = (r1 - mid.astype(F32)).astype(BF16)
    zero = jnp.zeros_like(hi)
    return jnp.where(lane < N_HEADS, hi,
                     jnp.where(lane < 2 * N_HEADS, mid,
                               jnp.where(lane < 3 * N_HEADS, lo, zero)))


def _chunk_cumsum(a_q, sub):
    rows = [a_q[v * V7X_SUBLANES:(v + 1) * V7X_SUBLANES, :] for v in range(ROW_GROUPS)]
    d = 1
    while d < ROW_GROUPS:
        wrapped = [_shift_down(rows[v], 1, sub) for v in range(ROW_GROUPS - d, ROW_GROUPS)]
        rows = [rows[v] + (rows[v - d] if v >= d else wrapped[v]) for v in range(ROW_GROUPS)]
        d *= 2
    j = 1
    while j < V7X_SUBLANES:
        rows = [r + _shift_down(r, j, sub) for r in rows]
        j *= 2
    return jnp.concatenate(rows, axis=0)


def _ssm_kernel(h_ref, g_ref, wzx_ref, wdt_ref, cw_ref, cb_ref, dtb_ref, alog_ref, dsk_ref,
                gn_ref, wout_ref, e64_ref, o_ref, tail_ref, st_ref, y_ref, *, tm):
    nblk = tm // CHUNK

    @pl.when(pl.program_id(1) == 0)
    def _():
        tail_ref[...] = jnp.zeros(tail_ref.shape, F32)
        st_ref[...] = jnp.zeros(st_ref.shape, F32)

    x = h_ref[...]
    xn = _rms_scale(x, g_ref[...]).astype(BF16)
    zx = _dot(xn, wzx_ref[...])
    z = zx[:, :D_INNER]
    u = zx[:, D_INNER:]

    sub_c = lax.broadcasted_iota(jnp.int32, (V7X_SUBLANES, CONV_DIM), 0)
    prev_tail = [tail_ref[i * V7X_SUBLANES:(i + 1) * V7X_SUBLANES, :] for i in range(TAIL_GROUPS)]
    conv_rows = []
    for c in range(nblk):
        grp = [u[c * CHUNK + v * V7X_SUBLANES:c * CHUNK + (v + 1) * V7X_SUBLANES, :]
               for v in range(ROW_GROUPS)]
        tail = grp[ROW_GROUPS - TAIL_GROUPS:]
        wrapped = [jnp.where(sub_c >= 1, pltpu.roll(tail[i], 1, 0), pltpu.roll(prev_tail[i], 1, 0))
                   for i in range(TAIL_GROUPS)]
        for v in range(ROW_GROUPS):
            acc = cb_ref[...]
            for d in range(SSM_CONV):
                src = grp[v - d] if v >= d else wrapped[v - d + TAIL_GROUPS]
                acc = acc + cw_ref[SSM_CONV - 1 - d:SSM_CONV - d, :] * src
            conv_rows.append(acc)
        prev_tail = tail
    for i in range(TAIL_GROUPS):
        tail_ref[i * V7X_SUBLANES:(i + 1) * V7X_SUBLANES, :] = prev_tail[i]
    xbc = _silu(jnp.concatenate(conv_rows, axis=0))
    xs = xbc[:, :D_INNER]
    bm = xbc[:, D_INNER:D_INNER + N_GROUPS * D_STATE]
    cm = xbc[:, D_INNER + N_GROUPS * D_STATE:]

    dtr = _dot(xn, wdt_ref[...]) + dtb_ref[...]
    dt = jnp.maximum(dtr, 0.0) + jnp.log(1.0 + jnp.exp(-jnp.abs(dtr)))
    a = dt * (-jnp.exp(alog_ref[...]))

    sub = lax.broadcasted_iota(jnp.int32, (V7X_SUBLANES, DT_WIDTH), 0)
    lane = lax.broadcasted_iota(jnp.int32, (CHUNK, V7X_LANES), 1)
    low = lane < HEAD_DIM
    low1 = lax.broadcasted_iota(jnp.int32, (1, V7X_LANES), 1) < HEAD_DIM
    lo64 = lax.broadcasted_iota(jnp.int32, (HALF_CHUNK, V7X_LANES), 1) < HEAD_DIM
    pos_r = lax.broadcasted_iota(jnp.int32, (CHUNK, 2 * CHUNK), 0)
    kcol = lax.broadcasted_iota(jnp.int32, (CHUNK, 2 * CHUNK), 1)
    pos_c = HALF_CHUNK * (kcol // CHUNK) + (kcol % HALF_CHUNK)
    time_r = ROW_GROUPS * (pos_r % V7X_SUBLANES) + pos_r // V7X_SUBLANES
    time_c = ROW_GROUPS * (pos_c % V7X_SUBLANES) + pos_c // V7X_SUBLANES
    causal = time_c <= time_r
    e64 = e64_ref[...]

    for q in range(nblk):
        r0 = q * CHUNK
        acs = _chunk_cumsum(a[r0:r0 + CHUNK, :], sub)
        acs_t = acs.T
        acs_tr = pltpu.roll(acs_t, HALF_CHUNK, 1)
        packed = jnp.concatenate([_split_pack(dt[r0:r0 + CHUNK, :], lane),
                                  _split_pack(acs, lane)], axis=0)
        spread = _dot(packed, e64)
        dt64 = spread[:CHUNK]
        acs64 = spread[CHUNK:]
        last64 = acs64[CHUNK - 1:CHUNK, :]
        xdt = xs[r0:r0 + CHUNK, :] * dt64
        xdec = (xdt * jnp.exp(last64 - acs64)).astype(BF16)
        dec_out = jnp.exp(acs64)
        dec_tot = jnp.exp(last64)

        for g in range(N_GROUPS):
            gs = slice(g * GROUP_WIDTH, (g + 1) * GROUP_WIDTH)
            b_gt = bm[r0:r0 + CHUNK, g * D_STATE:(g + 1) * D_STATE].T
            c_g = cm[r0:r0 + CHUNK, g * D_STATE:(g + 1) * D_STATE].astype(BF16)
            b_gtr = pltpu.roll(b_gt, HALF_CHUNK, 1)
            b_dup = jnp.concatenate([jnp.where(low, b_gt, b_gtr),
                                     jnp.where(low, b_gtr, b_gt)], axis=1).astype(BF16)
            cb = _dot(c_g, b_dup)
            st = st_ref[g]
            y_off = _dot(c_g, st.astype(BF16)) * dec_out[:, gs]
            st_ref[g] = st * dec_tot[:, gs] + _dot(b_gt.astype(BF16), xdec[:, gs])

            for jj in range(PAIRS_PER_GROUP):
                pair = g * PAIRS_PER_GROUP + jj
                h0, h1 = 2 * pair, 2 * pair + 1
                ps = slice(pair * PAIR_WIDTH, (pair + 1) * PAIR_WIDTH)
                col = acs64[:, ps]
                row_a = jnp.where(low1, acs_t[h0:h0 + 1, :], acs_tr[h1:h1 + 1, :])
                row_b = jnp.where(low1, acs_tr[h0:h0 + 1, :], acs_t[h1:h1 + 1, :])
                seg = jnp.concatenate([col - row_a, col - row_b], axis=1)
                m = (cb * jnp.exp(jnp.where(causal, seg, MASKED_LOG))).astype(BF16)
                xp = xdt[:, ps]
                top, bot = xp[:HALF_CHUNK], xp[HALF_CHUNK:]
                rhs = jnp.concatenate([jnp.where(lo64, top, 0.0), jnp.where(lo64, 0.0, top),
                                       jnp.where(lo64, bot, 0.0), jnp.where(lo64, 0.0, bot)],
                                      axis=0).astype(BF16)
                y_ref[r0:r0 + CHUNK, ps] = (
                    _dot(m, rhs) + y_off[:, jj * PAIR_WIDTH:(jj + 1) * PAIR_WIDTH])
    y = y_ref[...] + xs * dsk_ref[...]
    hg = y * _silu(z)
    normed = []
    for g in range(N_GROUPS):
        hs = hg[:, g * GROUP_WIDTH:(g + 1) * GROUP_WIDTH]
        ms = jnp.mean(hs * hs, axis=-1, keepdims=True)
        normed.append(hs * lax.rsqrt(ms + EPS))
    hn = (jnp.concatenate(normed, axis=1) * gn_ref[...]).astype(BF16)
    o_ref[...] = x + _dot(hn, wout_ref[...])


def _expander(width):
    e = np.zeros((DT_WIDTH, N_HEADS * width), np.float32)
    for r in range(HEAD_REPLICAS):
        for h in range(N_HEADS):
            e[r * N_HEADS + h, h * width:(h + 1) * width] = 1.0
    return jnp.asarray(e, BF16)


def _replicate_heads(v):
    rep = jnp.tile(v, HEAD_REPLICAS)
    return jnp.pad(rep, (0, DT_WIDTH - rep.shape[0]))[None, :]


def _ssm_layer(h, layer, g, w_in_bf, w_in, conv_w, conv_b, dt_bias, a_log, d_skip, gate_norm,
               w_out_bf, *, tm):
    b, l, d = h.shape
    n_zx = D_INNER + CONV_DIM
    wdt = jnp.pad(jnp.tile(w_in[:, n_zx:], (1, HEAD_REPLICAS)),
                  ((0, 0), (0, DT_WIDTH - HEAD_REPLICAS * N_HEADS))).astype(BF16)
    args = (g[None, :], w_in_bf, wdt, conv_w, conv_b[None, :], _replicate_heads(dt_bias),
            _replicate_heads(a_log), jnp.repeat(d_skip, HEAD_DIM)[None, :], gate_norm[None, :],
            w_out_bf, _expander(HEAD_DIM))
    specs = [_resident(a.shape) for a in args]
    specs[1] = _layer_of(w_in_bf, layer, (d, n_zx))
    specs[9] = _layer_of(w_out_bf, layer)
    tile = pl.BlockSpec((None, tm, d), lambda i, j: (i, j, 0))
    return pl.pallas_call(
        functools.partial(_ssm_kernel, tm=tm),
        grid=(b, l // tm),
        in_specs=[tile] + specs,
        out_specs=tile,
        out_shape=jax.ShapeDtypeStruct(h.shape, h.dtype),
        scratch_shapes=[pltpu.VMEM((TAIL_GROUPS * V7X_SUBLANES, CONV_DIM), F32),
                        pltpu.VMEM((N_GROUPS, D_STATE, GROUP_WIDTH), F32),
                        pltpu.VMEM((tm, D_INNER), F32)],
        compiler_params=_params(2),
        name="ssd_mixer",
    )(h, *args)


SSM_TILE = 256
CONF_TILE = 512
FFN_TILE = 512


def kernel(x, ssm_norm, ssm_w_in, ssm_conv_w, ssm_conv_b, ssm_dt_bias, ssm_a_log, ssm_d, ssm_gate_norm, ssm_w_out, cv_norm, cv_w_pw1, cv_b_pw1, cv_dw_w, cv_dw_b, cv_ln_g, cv_ln_b, cv_w_pw2, cv_b_pw2, ffn_norm, ffn_w_gate, ffn_w_up, ffn_w_down, final_norm):
    w_in, w_out = _to_bf16(ssm_w_in), _to_bf16(ssm_w_out)
    w_pw1, w_pw2 = _to_bf16(cv_w_pw1), _to_bf16(cv_w_pw2)
    w_gate, w_up, w_down = _to_bf16(ffn_w_gate), _to_bf16(ffn_w_up), _to_bf16(ffn_w_down)
    h = _interleave(x)
    for i in range(DEPTH):
        j = i // 2
        if i % 2 == 0:
            h = _ssm_layer(h, j, ssm_norm[j], w_in, ssm_w_in[j], ssm_conv_w[j], ssm_conv_b[j],
                           ssm_dt_bias[j], ssm_a_log[j], ssm_d[j], ssm_gate_norm[j], w_out,
                           tm=SSM_TILE)
        else:
            h = _conf_layer(h, j, cv_norm[j][None, :], w_pw1, cv_b_pw1[j][None, :], cv_dw_w[j],
                            cv_dw_b[j][None, :], cv_ln_g[j][None, :], cv_ln_b[j][None, :],
                            w_pw2, cv_b_pw2[j][None, :], tm=CONF_TILE)
        h = _ffn_layer(h, i, ffn_norm[i][None, :], w_gate, w_up, w_down, final_norm[None, :],
                       final_norm=(i == DEPTH - 1), tm=FFN_TILE)
    return _deinterleave(h)
```

```python
import functools

import numpy as np
import jax
import jax.numpy as jnp
from jax import lax
from jax.experimental import pallas as pl
from jax.experimental.pallas import tpu as pltpu

D_MODEL = 1024
DEPTH = 4
D_INNER = 2048
HEAD_DIM = 64
N_HEADS = 32
N_GROUPS = 4
HEADS_PER_GROUP = 8
D_STATE = 128
SSM_CONV = 4
CONV_DIM = D_INNER + 2 * N_GROUPS * D_STATE
CHUNK = 128
CONV_KERNEL = 31
D_FF = 2816
EPS = 1e-5

V7X_LANES = 128
V7X_SUBLANES = 8
V7X_VMEM_LIMIT_BYTES = 56 * 1024 * 1024

ROW_GROUPS = CHUNK // V7X_SUBLANES
GROUP_WIDTH = HEADS_PER_GROUP * HEAD_DIM
HEAD_REPLICAS = 3
PAIR_WIDTH = 2 * HEAD_DIM
PAIRS_PER_GROUP = HEADS_PER_GROUP // 2
HALF_CHUNK = CHUNK // 2
MASKED_LOG = -1e30
LANE_BLOCKS = D_MODEL // V7X_LANES
CAST_BLOCK_BYTES = 12 * 1024 * 1024
CAST_ROW_ALIGN = 16

F32 = jnp.float32
BF16 = jnp.bfloat16


def _sigmoid(x):
    return 1.0 / (1.0 + jnp.exp(-x))


def _silu(x):
    return x * _sigmoid(x)


def _rms_scale(x, g):
    ms = jnp.mean(x * x, axis=-1, keepdims=True)
    return x * lax.rsqrt(ms + EPS) * g


def _dot(a, b):
    return jnp.dot(a, b, preferred_element_type=F32)


def _resident(shape):
    nd = len(shape)
    return pl.BlockSpec(shape, lambda *_: (0,) * nd, pipeline_mode=pl.Buffered(1))


def _layer_of(stacked, layer, shape=None):
    shape = tuple(stacked.shape[1:]) if shape is None else shape
    nd = len(shape)
    return pl.BlockSpec((None,) + shape, lambda *_: (layer,) + (0,) * nd,
                        pipeline_mode=pl.Buffered(1))


def _cast_kernel(x_ref, o_ref):
    o_ref[...] = x_ref[...].astype(o_ref.dtype)


def _to_bf16(w):
    n, r, c = w.shape
    parts = 1
    while r % parts or (r // parts) % CAST_ROW_ALIGN or (r // parts) * c * 4 > CAST_BLOCK_BYTES:
        parts += 1
    rows = r // parts
    spec = pl.BlockSpec((None, rows, c), lambda i, j: (i, j, 0))
    return pl.pallas_call(
        _cast_kernel,
        grid=(n, parts),
        in_specs=[spec],
        out_specs=spec,
        out_shape=jax.ShapeDtypeStruct(w.shape, BF16),
        compiler_params=_params(2),
        name="cast_bf16",
    )(w)


def _params(n_grid):
    return pltpu.CompilerParams(
        dimension_semantics=("arbitrary",) * n_grid,
        vmem_limit_bytes=V7X_VMEM_LIMIT_BYTES)


def _deinterleave(x):
    b, l, d = x.shape
    x = x.reshape(b, l // CHUNK, ROW_GROUPS, V7X_SUBLANES, d)
    return x.swapaxes(2, 3).reshape(b, l, d)


def _shift_down(x, j, sub):
    return jnp.where(sub >= j, pltpu.roll(x, j, 0), 0.0)


def _ffn_kernel(h_ref, g_ref, wg_ref, wu_ref, wd_ref, fg_ref, o_ref, *, final_norm):
    x = h_ref[...]
    xn = _rms_scale(x, g_ref[...]).astype(BF16)
    gate = _dot(xn, wg_ref[...])
    up = _dot(xn, wu_ref[...])
    act = (_silu(gate) * up).astype(BF16)
    y = x + _dot(act, wd_ref[...])
    if final_norm:
        y = _rms_scale(y, fg_ref[...])
    o_ref[...] = y


def _ffn_layer(h, layer, g, wg, wu, wd, fg, *, final_norm, tm):
    b, l, d = h.shape
    tile = pl.BlockSpec((None, tm, d), lambda i, j: (i, j, 0))
    return pl.pallas_call(
        functools.partial(_ffn_kernel, final_norm=final_norm),
        grid=(b, l // tm),
        in_specs=[tile, _resident(g.shape), _layer_of(wg, layer), _layer_of(wu, layer),
                  _layer_of(wd, layer), _resident(fg.shape)],
        out_specs=tile,
        out_shape=jax.ShapeDtypeStruct(h.shape, h.dtype),
        compiler_params=_params(2),
        name="swiglu_ffn",
    )(h, g, wg, wu, wd, fg)


def _conf_kernel(h_ref, g_ref, w1_ref, b1_ref, dww_ref, dwb_ref, lng_ref, lnb_ref,
                 w2_ref, b2_ref, o_ref, u0_ref, u1_ref, u2_ref, acc_ref, w1b_ref, w2b_ref,
                 *, tm):
    nblk = tm // CHUNK

    @pl.when((pl.program_id(0) == 0) & (pl.program_id(1) == 0))
    def _():
        w1b_ref[...] = w1_ref[...].astype(BF16)
        w2b_ref[...] = w2_ref[...].astype(BF16)

    @pl.when(pl.program_id(1) == 0)
    def _():
        u0_ref[:, 0:CHUNK, :] = jnp.zeros((LANE_BLOCKS, CHUNK, V7X_LANES), F32)

    x = h_ref[...]
    xn = _rms_scale(x, g_ref[...]).astype(BF16)
    hh = _dot(xn, w1b_ref[...]) + b1_ref[...]
    u = hh[:, :D_MODEL] * _sigmoid(hh[:, D_MODEL:])
    for lb in range(LANE_BLOCKS):
        u0_ref[lb, CHUNK:CHUNK + tm, :] = u[:, lb * V7X_LANES:(lb + 1) * V7X_LANES]

    sub = lax.broadcasted_iota(jnp.int32, (V7X_SUBLANES, V7X_LANES), 0)

    def lane_block(lb, carry):
        def group(ref, c, v):
            r = c * CHUNK + v * V7X_SUBLANES
            return ref[lb, r:r + V7X_SUBLANES, :]

        prev1 = [pltpu.roll(group(u0_ref, 0, v), 1, 0) for v in range(ROW_GROUPS)]
        prev2 = [pltpu.roll(group(u0_ref, 0, v), 2, 0) for v in range(ROW_GROUPS)]
        for c in range(nblk):
            for v in range(ROW_GROUPS):
                cur = group(u0_ref, c + 1, v)
                r1 = pltpu.roll(cur, 1, 0)
                r2 = pltpu.roll(cur, 2, 0)
                r = c * CHUNK + v * V7X_SUBLANES
                u1_ref[lb, r:r + V7X_SUBLANES, :] = jnp.where(sub >= 1, r1, prev1[v])
                u2_ref[lb, r:r + V7X_SUBLANES, :] = jnp.where(sub >= 2, r2, prev2[v])
                prev1[v] = r1
                prev2[v] = r2

        w = [jnp.broadcast_to(dww_ref[lb, k:k + 1, :], (V7X_SUBLANES, V7X_LANES))
             for k in range(CONV_KERNEL)]
        bias = jnp.broadcast_to(dwb_ref[lb], (V7X_SUBLANES, V7X_LANES))
        for c in range(nblk):
            for v in range(ROW_GROUPS):
                acc = bias
                for d in range(CONV_KERNEL):
                    a, r = divmod(d, ROW_GROUPS)
                    vv, j = (v - r, a) if v >= r else (v - r + ROW_GROUPS, a + 1)
                    if j == 0:
                        src = group(u0_ref, c + 1, vv)
                    elif j == 1:
                        src = group(u1_ref, c, vv)
                    else:
                        src = group(u2_ref, c, vv)
                    acc = acc + w[CONV_KERNEL - 1 - d] * src
                r = c * CHUNK + v * V7X_SUBLANES
                acc_ref[lb, r:r + V7X_SUBLANES, :] = acc
        return carry

    lax.fori_loop(0, LANE_BLOCKS, lane_block, 0)
    u0_ref[:, 0:CHUNK, :] = u0_ref[:, tm:tm + CHUNK, :]

    conv = jnp.concatenate([acc_ref[lb] for lb in range(LANE_BLOCKS)], axis=1)
    mu = jnp.mean(conv, axis=-1, keepdims=True)
    xc = conv - mu
    var = jnp.mean(xc * xc, axis=-1, keepdims=True)
    y = xc * lax.rsqrt(var + EPS) * lng_ref[...] + lnb_ref[...]
    s = _silu(y).astype(BF16)
    o_ref[...] = x + _dot(s, w2b_ref[...]) + b2_ref[...]


def _lane_blocked(w):
    return w.reshape(w.shape[0], LANE_BLOCKS, V7X_LANES).swapaxes(0, 1)


def _conf_layer(h, layer, g, w1, b1, dww, dwb, lng, lnb, w2, b2, *, tm):
    b, l, d = h.shape
    tile = pl.BlockSpec((None, tm, d), lambda i, j: (i, j, 0))
    args = (g, w1, b1, _lane_blocked(dww), _lane_blocked(dwb), lng, lnb, w2, b2)
    specs = [_layer_of(a, layer) if a is w1 or a is w2 else _resident(a.shape) for a in args]
    blocked = functools.partial(pltpu.VMEM, dtype=F32)
    return pl.pallas_call(
        functools.partial(_conf_kernel, tm=tm),
        grid=(b, l // tm),
        in_specs=[tile] + specs,
        out_specs=tile,
        out_shape=jax.ShapeDtypeStruct(h.shape, h.dtype),
        scratch_shapes=[blocked((LANE_BLOCKS, CHUNK + tm, V7X_LANES)),
                        blocked((LANE_BLOCKS, tm, V7X_LANES)),
                        blocked((LANE_BLOCKS, tm, V7X_LANES)),
                        blocked((LANE_BLOCKS, tm, V7X_LANES)),
                        pltpu.VMEM(w1.shape[1:], BF16),
                        pltpu.VMEM(w2.shape[1:], BF16)],
        compiler_params=_params(2),
        name="conformer_conv",
    )(h, *args)


DT_WIDTH = V7X_LANES
TAIL_GROUPS = SSM_CONV - 1


def _split_pack(v, lane):
    hi = v.astype(BF16)
    r1 = v - hi.astype(F32)
    mid = r1.astype(BF16)
    lo = (r1 - mid.astype(F32)).astype(BF16)
    zero = jnp.zeros_like(hi)
    return jnp.where(lane < N_HEADS, hi,
                     jnp.where(lane < 2 * N_HEADS, mid,
                               jnp.where(lane < 3 * N_HEADS, lo, zero)))


def _chunk_cumsum(a_q, sub):
    rows = [a_q[v * V7X_SUBLANES:(v + 1) * V7X_SUBLANES, :] for v in range(ROW_GROUPS)]
    d = 1
    while d < ROW_GROUPS:
        wrapped = [_shift_down(rows[v], 1, sub) for v in range(ROW_GROUPS - d, ROW_GROUPS)]
        rows = [rows[v] + (rows[v - d] if v >= d else wrapped[v]) for v in range(ROW_GROUPS)]
        d *= 2
    j = 1
    while j < V7X_SUBLANES:
        rows = [r + _shift_down(r, j, sub) for r in rows]
        j *= 2
    return jnp.concatenate(rows, axis=0)


def _ssm_kernel(*refs, tm, n_in):
    h_refs = refs[:n_in]
    (g_ref, wzx_ref, wdt_ref, cw_ref, cb_ref, dtb_ref, alog_ref, dsk_ref, gn_ref, wout_ref,
     e64_ref, o_ref, tail_ref, st_ref, y_ref) = refs[n_in:]
    nblk = tm // CHUNK

    @pl.when(pl.program_id(1) == 0)
    def _():
        tail_ref[...] = jnp.zeros(tail_ref.shape, F32)
        st_ref[...] = jnp.zeros(st_ref.shape, F32)

    if n_in == 1:
        x = h_refs[0][...]
    else:
        x = jnp.concatenate([h_refs[v][c] for c in range(nblk) for v in range(ROW_GROUPS)], axis=0)
    xn = _rms_scale(x, g_ref[...]).astype(BF16)
    zx = _dot(xn, wzx_ref[...])
    z = zx[:, :D_INNER]
    u = zx[:, D_INNER:]

    sub_c = lax.broadcasted_iota(jnp.int32, (V7X_SUBLANES, CONV_DIM), 0)
    prev_tail = [tail_ref[i * V7X_SUBLANES:(i + 1) * V7X_SUBLANES, :] for i in range(TAIL_GROUPS)]
    conv_rows = []
    for c in range(nblk):
        grp = [u[c * CHUNK + v * V7X_SUBLANES:c * CHUNK + (v + 1) * V7X_SUBLANES, :]
               for v in range(ROW_GROUPS)]
        tail = grp[ROW_GROUPS - TAIL_GROUPS:]
        wrapped = [jnp.where(sub_c >= 1, pltpu.roll(tail[i], 1, 0), pltpu.roll(prev_tail[i], 1, 0))
                   for i in range(TAIL_GROUPS)]
        for v in range(ROW_GROUPS):
            acc = cb_ref[...]
            for d in range(SSM_CONV):
                src = grp[v - d] if v >= d else wrapped[v - d + TAIL_GROUPS]
                acc = acc + cw_ref[SSM_CONV - 1 - d:SSM_CONV - d, :] * src
            conv_rows.append(acc)
        prev_tail = tail
    for i in range(TAIL_GROUPS):
        tail_ref[i * V7X_SUBLANES:(i + 1) * V7X_SUBLANES, :] = prev_tail[i]
    xbc = _silu(jnp.concatenate(conv_rows, axis=0))
    xs = xbc[:, :D_INNER]
    bm = xbc[:, D_INNER:D_INNER + N_GROUPS * D_STATE]
    cm = xbc[:, D_INNER + N_GROUPS * D_STATE:]

    dtr = _dot(xn, wdt_ref[...]) + dtb_ref[...]
    dt = jnp.maximum(dtr, 0.0) + jnp.log(1.0 + jnp.exp(-jnp.abs(dtr)))
    a = dt * (-jnp.exp(alog_ref[...]))

    sub = lax.broadcasted_iota(jnp.int32, (V7X_SUBLANES, DT_WIDTH), 0)
    lane = lax.broadcasted_iota(jnp.int32, (CHUNK, V7X_LANES), 1)
    low = lane < HEAD_DIM
    low1 = lax.broadcasted_iota(jnp.int32, (1, V7X_LANES), 1) < HEAD_DIM
    lo64 = lax.broadcasted_iota(jnp.int32, (HALF_CHUNK, V7X_LANES), 1) < HEAD_DIM
    pos_r = lax.broadcasted_iota(jnp.int32, (CHUNK, 2 * CHUNK), 0)
    kcol = lax.broadcasted_iota(jnp.int32, (CHUNK, 2 * CHUNK), 1)
    pos_c = HALF_CHUNK * (kcol // CHUNK) + (kcol % HALF_CHUNK)
    time_r = ROW_GROUPS * (pos_r % V7X_SUBLANES) + pos_r // V7X_SUBLANES
    time_c = ROW_GROUPS * (pos_c % V7X_SUBLANES) + pos_c // V7X_SUBLANES
    causal = time_c <= time_r
    e64 = e64_ref[...]

    for q in range(nblk):
        r0 = q * CHUNK
        acs = _chunk_cumsum(a[r0:r0 + CHUNK, :], sub)
        acs_t = acs.T
        acs_tr = pltpu.roll(acs_t, HALF_CHUNK, 1)
        packed = jnp.concatenate([_split_pack(dt[r0:r0 + CHUNK, :], lane),
                                  _split_pack(acs, lane)], axis=0)
        spread = _dot(packed, e64)
        dt64 = spread[:CHUNK]
        acs64 = spread[CHUNK:]
        last64 = acs64[CHUNK - 1:CHUNK, :]
        xdt = xs[r0:r0 + CHUNK, :] * dt64
        xdec = (xdt * jnp.exp(last64 - acs64)).astype(BF16)
        dec_out = jnp.exp(acs64)
        dec_tot = jnp.exp(last64)

        for g in range(N_GROUPS):
            gs = slice(g * GROUP_WIDTH, (g + 1) * GROUP_WIDTH)
            b_gt = bm[r0:r0 + CHUNK, g * D_STATE:(g + 1) * D_STATE].T
            c_g = cm[r0:r0 + CHUNK, g * D_STATE:(g + 1) * D_STATE].astype(BF16)
            b_gtr = pltpu.roll(b_gt, HALF_CHUNK, 1)
            b_dup = jnp.concatenate([jnp.where(low, b_gt, b_gtr),
                                     jnp.where(low, b_gtr, b_gt)], axis=1).astype(BF16)
            cb = _dot(c_g, b_dup)
            st = st_ref[g]
            y_off = _dot(c_g, st.astype(BF16)) * dec_out[:, gs]
            st_ref[g] = st * dec_tot[:, gs] + _dot(b_gt.astype(BF16), xdec[:, gs])

            for jj in range(PAIRS_PER_GROUP):
                pair = g * PAIRS_PER_GROUP + jj
                h0, h1 = 2 * pair, 2 * pair + 1
                ps = slice(pair * PAIR_WIDTH, (pair + 1) * PAIR_WIDTH)
                col = acs64[:, ps]
                row_a = jnp.where(low1, acs_t[h0:h0 + 1, :], acs_tr[h1:h1 + 1, :])
                row_b = jnp.where(low1, acs_tr[h0:h0 + 1, :], acs_t[h1:h1 + 1, :])
                seg = jnp.concatenate([col - row_a, col - row_b], axis=1)
                m = (cb * jnp.exp(jnp.where(causal, seg, MASKED_LOG))).astype(BF16)
                xp = xdt[:, ps]
                top, bot = xp[:HALF_CHUNK], xp[HALF_CHUNK:]
                rhs = jnp.concatenate([jnp.where(lo64, top, 0.0), jnp.where(lo64, 0.0, top),
                                       jnp.where(lo64, bot, 0.0), jnp.where(lo64, 0.0, bot)],
                                      axis=0).astype(BF16)
                y_ref[r0:r0 + CHUNK, ps] = (
                    _dot(m, rhs) + y_off[:, jj * PAIR_WIDTH:(jj + 1) * PAIR_WIDTH])
    y = y_ref[...] + xs * dsk_ref[...]
    hg = y * _silu(z)
    normed = []
    for g in range(N_GROUPS):
        hs = hg[:, g * GROUP_WIDTH:(g + 1) * GROUP_WIDTH]
        ms = jnp.mean(hs * hs, axis=-1, keepdims=True)
        normed.append(hs * lax.rsqrt(ms + EPS))
    hn = (jnp.concatenate(normed, axis=1) * gn_ref[...]).astype(BF16)
    o_ref[...] = x + _dot(hn, wout_ref[...])


def _expander(width):
    e = np.zeros((DT_WIDTH, N_HEADS * width), np.float32)
    for r in range(HEAD_REPLICAS):
        for h in range(N_HEADS):
            e[r * N_HEADS + h, h * width:(h + 1) * width] = 1.0
    return jnp.asarray(e, BF16)


def _replicate_heads(v):
    rep = jnp.tile(v, HEAD_REPLICAS)
    return jnp.pad(rep, (0, DT_WIDTH - rep.shape[0]))[None, :]


def _ssm_layer(h, layer, g, w_in_bf, w_in, conv_w, conv_b, dt_bias, a_log, d_skip, gate_norm,
               w_out_bf, *, tm, natural_order):
    b, l, d = h.shape
    n_zx = D_INNER + CONV_DIM
    wdt = jnp.pad(jnp.tile(w_in[:, n_zx:], (1, HEAD_REPLICAS)),
                  ((0, 0), (0, DT_WIDTH - HEAD_REPLICAS * N_HEADS))).astype(BF16)
    args = (g[None, :], w_in_bf, wdt, conv_w, conv_b[None, :], _replicate_heads(dt_bias),
            _replicate_heads(a_log), jnp.repeat(d_skip, HEAD_DIM)[None, :], gate_norm[None, :],
            w_out_bf, _expander(HEAD_DIM))
    specs = [_resident(a.shape) for a in args]
    specs[1] = _layer_of(w_in_bf, layer, (d, n_zx))
    specs[9] = _layer_of(w_out_bf, layer)
    tile = pl.BlockSpec((None, tm, d), lambda i, j: (i, j, 0))
    if natural_order:
        h_in = (h.reshape(b, l // CHUNK, V7X_SUBLANES, ROW_GROUPS * d),) * ROW_GROUPS
        h_specs = [pl.BlockSpec((None, tm // CHUNK, V7X_SUBLANES, d),
                                lambda i, j, v=v: (i, j, 0, v)) for v in range(ROW_GROUPS)]
    else:
        h_in, h_specs = (h,), [tile]
    return pl.pallas_call(
        functools.partial(_ssm_kernel, tm=tm, n_in=len(h_in)),
        grid=(b, l // tm),
        in_specs=h_specs + specs,
        out_specs=tile,
        out_shape=jax.ShapeDtypeStruct(h.shape, h.dtype),
        scratch_shapes=[pltpu.VMEM((TAIL_GROUPS * V7X_SUBLANES, CONV_DIM), F32),
                        pltpu.VMEM((N_GROUPS, D_STATE, GROUP_WIDTH), F32),
                        pltpu.VMEM((tm, D_INNER), F32)],
        compiler_params=_params(2),
        name="ssd_mixer",
    )(*h_in, *args)


SSM_TILE = 256
CONF_TILE = 512
FFN_TILE = 512


def kernel(x, ssm_norm, ssm_w_in, ssm_conv_w, ssm_conv_b, ssm_dt_bias, ssm_a_log, ssm_d, ssm_gate_norm, ssm_w_out, cv_norm, cv_w_pw1, cv_b_pw1, cv_dw_w, cv_dw_b, cv_ln_g, cv_ln_b, cv_w_pw2, cv_b_pw2, ffn_norm, ffn_w_gate, ffn_w_up, ffn_w_down, final_norm):
    w_in, w_out = _to_bf16(ssm_w_in), _to_bf16(ssm_w_out)
    w_gate, w_up, w_down = _to_bf16(ffn_w_gate), _to_bf16(ffn_w_up), _to_bf16(ffn_w_down)
    h = x
    for i in range(DEPTH):
        j = i // 2
        if i % 2 == 0:
            h = _ssm_layer(h, j, ssm_norm[j], w_in, ssm_w_in[j], ssm_conv_w[j], ssm_conv_b[j],
                           ssm_dt_bias[j], ssm_a_log[j], ssm_d[j], ssm_gate_norm[j], w_out,
                           tm=SSM_TILE, natural_order=(i == 0))
        else:
            h = _conf_layer(h, j, cv_norm[j][None, :], cv_w_pw1, cv_b_pw1[j][None, :], cv_dw_w[j],
                            cv_dw_b[j][None, :], cv_ln_g[j][None, :], cv_ln_b[j][None, :],
                            cv_w_pw2, cv_b_pw2[j][None, :], tm=CONF_TILE)
        h = _ffn_layer(h, i, ffn_norm[i][None, :], w_gate, w_up, w_down, final_norm[None, :],
                       final_norm=(i == DEPTH - 1), tm=FFN_TILE)
    return _deinterleave(h)
```

```python
import functools

import numpy as np
import jax
import jax.numpy as jnp
from jax import lax
from jax.experimental import pallas as pl
from jax.experimental.pallas import tpu as pltpu

D_MODEL = 1024
DEPTH = 4
D_INNER = 2048
HEAD_DIM = 64
N_HEADS = 32
N_GROUPS = 4
HEADS_PER_GROUP = 8
D_STATE = 128
SSM_CONV = 4
CONV_DIM = D_INNER + 2 * N_GROUPS * D_STATE
CHUNK = 128
CONV_KERNEL = 31
D_FF = 2816
EPS = 1e-5

V7X_LANES = 128
V7X_SUBLANES = 8
V7X_VMEM_LIMIT_BYTES = 56 * 1024 * 1024

ROW_GROUPS = CHUNK // V7X_SUBLANES
GROUP_WIDTH = HEADS_PER_GROUP * HEAD_DIM
HEAD_REPLICAS = 3
PAIR_WIDTH = 2 * HEAD_DIM
PAIRS_PER_GROUP = HEADS_PER_GROUP // 2
HALF_CHUNK = CHUNK // 2
MASKED_LOG = -1e30
LANE_BLOCKS = D_MODEL // V7X_LANES
CAST_BLOCK_BYTES = 12 * 1024 * 1024
CAST_ROW_ALIGN = 16

F32 = jnp.float32
BF16 = jnp.bfloat16


def _sigmoid(x):
    return 1.0 / (1.0 + jnp.exp(-x))


def _silu(x):
    return x * _sigmoid(x)


def _rms_scale(x, g):
    ms = jnp.mean(x * x, axis=-1, keepdims=True)
    return x * lax.rsqrt(ms + EPS) * g


def _dot(a, b):
    return jnp.dot(a, b, preferred_element_type=F32)


def _resident(shape):
    nd = len(shape)
    return pl.BlockSpec(shape, lambda *_: (0,) * nd, pipeline_mode=pl.Buffered(1))


def _layer_of(stacked, layer, shape=None):
    shape = tuple(stacked.shape[1:]) if shape is None else shape
    nd = len(shape)
    return pl.BlockSpec((None,) + shape, lambda *_: (layer,) + (0,) * nd,
                        pipeline_mode=pl.Buffered(1))


def _cast_kernel(x_ref, o_ref):
    o_ref[...] = x_ref[...].astype(o_ref.dtype)


def _to_bf16(w):
    n, r, c = w.shape
    parts = 1
    while r % parts or (r // parts) % CAST_ROW_ALIGN or (r // parts) * c * 4 > CAST_BLOCK_BYTES:
        parts += 1
    rows = r // parts
    spec = pl.BlockSpec((None, rows, c), lambda i, j: (i, j, 0))
    return pl.pallas_call(
        _cast_kernel,
        grid=(n, parts),
        in_specs=[spec],
        out_specs=spec,
        out_shape=jax.ShapeDtypeStruct(w.shape, BF16),
        compiler_params=_params(2),
        name="cast_bf16",
    )(w)


def _params(n_grid):
    return pltpu.CompilerParams(
        dimension_semantics=("arbitrary",) * n_grid,
        vmem_limit_bytes=V7X_VMEM_LIMIT_BYTES)


def _interleave(x):
    b, l, d = x.shape
    x = x.reshape(b, l // CHUNK, V7X_SUBLANES, ROW_GROUPS, d)
    return x.swapaxes(2, 3).reshape(b, l, d)


def _deinterleave(x):
    b, l, d = x.shape
    x = x.reshape(b, l // CHUNK, ROW_GROUPS, V7X_SUBLANES, d)
    return x.swapaxes(2, 3).reshape(b, l, d)


def _shift_down(x, j, sub):
    return jnp.where(sub >= j, pltpu.roll(x, j, 0), 0.0)


def _ffn_kernel(h_ref, g_ref, wg_ref, wu_ref, wd_ref, fg_ref, o_ref, *, final_norm):
    x = h_ref[...]
    xn = _rms_scale(x, g_ref[...]).astype(BF16)
    gate = _dot(xn, wg_ref[...])
    up = _dot(xn, wu_ref[...])
    act = (_silu(gate) * up).astype(BF16)
    y = x + _dot(act, wd_ref[...])
    if final_norm:
        y = _rms_scale(y, fg_ref[...])
    o_ref[...] = y


def _ffn_layer(h, layer, g, wg, wu, wd, fg, *, final_norm, tm):
    b, l, d = h.shape
    tile = pl.BlockSpec((None, tm, d), lambda i, j: (i, j, 0))
    return pl.pallas_call(
        functools.partial(_ffn_kernel, final_norm=final_norm),
        grid=(b, l // tm),
        in_specs=[tile, _resident(g.shape), _layer_of(wg, layer), _layer_of(wu, layer),
                  _layer_of(wd, layer), _resident(fg.shape)],
        out_specs=tile,
        out_shape=jax.ShapeDtypeStruct(h.shape, h.dtype),
        compiler_params=_params(2),
        name="swiglu_ffn",
    )(h, g, wg, wu, wd, fg)


def _conf_kernel(h_ref, g_ref, w1_ref, b1_ref, dww_ref, dwb_ref, lng_ref, lnb_ref,
                 w2_ref, b2_ref, o_ref, u0_ref, u1_ref, u2_ref, acc_ref, w1b_ref, w2b_ref,
                 *, tm):
    nblk = tm // CHUNK

    @pl.when((pl.program_id(0) == 0) & (pl.program_id(1) == 0))
    def _():
        w1b_ref[...] = w1_ref[...].astype(BF16)
        w2b_ref[...] = w2_ref[...].astype(BF16)

    @pl.when(pl.program_id(1) == 0)
    def _():
        u0_ref[:, 0:CHUNK, :] = jnp.zeros((LANE_BLOCKS, CHUNK, V7X_LANES), F32)

    x = h_ref[...]
    xn = _rms_scale(x, g_ref[...]).astype(BF16)
    hh = _dot(xn, w1b_ref[...]) + b1_ref[...]
    u = hh[:, :D_MODEL] * _sigmoid(hh[:, D_MODEL:])
    for lb in range(LANE_BLOCKS):
        u0_ref[lb, CHUNK:CHUNK + tm, :] = u[:, lb * V7X_LANES:(lb + 1) * V7X_LANES]

    sub = lax.broadcasted_iota(jnp.int32, (V7X_SUBLANES, V7X_LANES), 0)

    def lane_block(lb, carry):
        def group(ref, c, v):
            r = c * CHUNK + v * V7X_SUBLANES
            return ref[lb, r:r + V7X_SUBLANES, :]

        prev1 = [pltpu.roll(group(u0_ref, 0, v), 1, 0) for v in range(ROW_GROUPS)]
        prev2 = [pltpu.roll(group(u0_ref, 0, v), 2, 0) for v in range(ROW_GROUPS)]
        for c in range(nblk):
            for v in range(ROW_GROUPS):
                cur = group(u0_ref, c + 1, v)
                r1 = pltpu.roll(cur, 1, 0)
                r2 = pltpu.roll(cur, 2, 0)
                r = c * CHUNK + v * V7X_SUBLANES
                u1_ref[lb, r:r + V7X_SUBLANES, :] = jnp.where(sub >= 1, r1, prev1[v])
                u2_ref[lb, r:r + V7X_SUBLANES, :] = jnp.where(sub >= 2, r2, prev2[v])
                prev1[v] = r1
                prev2[v] = r2

        w = [jnp.broadcast_to(dww_ref[lb, k:k + 1, :], (V7X_SUBLANES, V7X_LANES))
             for k in range(CONV_KERNEL)]
        bias = jnp.broadcast_to(dwb_ref[lb], (V7X_SUBLANES, V7X_LANES))
        for c in range(nblk):
            for v in range(ROW_GROUPS):
                acc = bias
                for d in range(CONV_KERNEL):
                    a, r = divmod(d, ROW_GROUPS)
                    vv, j = (v - r, a) if v >= r else (v - r + ROW_GROUPS, a + 1)
                    if j == 0:
                        src = group(u0_ref, c + 1, vv)
                    elif j == 1:
                        src = group(u1_ref, c, vv)
                    else:
                        src = group(u2_ref, c, vv)
                    acc = acc + w[CONV_KERNEL - 1 - d] * src
                r = c * CHUNK + v * V7X_SUBLANES
                acc_ref[lb, r:r + V7X_SUBLANES, :] = acc
        return carry

    lax.fori_loop(0, LANE_BLOCKS, lane_block, 0)
    u0_ref[:, 0:CHUNK, :] = u0_ref[:, tm:tm + CHUNK, :]

    conv = jnp.concatenate([acc_ref[lb] for lb in range(LANE_BLOCKS)], axis=1)
    mu = jnp.mean(conv, axis=-1, keepdims=True)
    xc = conv - mu
    var = jnp.mean(xc * xc, axis=-1, keepdims=True)
    y = xc * lax.rsqrt(var + EPS) * lng_ref[...] + lnb_ref[...]
    s = _silu(y).astype(BF16)
    o_ref[...] = x + _dot(s, w2b_ref[...]) + b2_ref[...]


def _lane_blocked(w):
    return w.reshape(w.shape[0], LANE_BLOCKS, V7X_LANES).swapaxes(0, 1)


def _conf_layer(h, layer, g, w1, b1, dww, dwb, lng, lnb, w2, b2, *, tm):
    b, l, d = h.shape
    tile = pl.BlockSpec((None, tm, d), lambda i, j: (i, j, 0))
    args = (g, w1, b1, _lane_blocked(dww), _lane_blocked(dwb), lng, lnb, w2, b2)
    specs = [_layer_of(a, layer) if a is w1 or a is w2 else _resident(a.shape) for a in args]
    blocked = functools.partial(pltpu.VMEM, dtype=F32)
    return pl.pallas_call(
        functools.partial(_conf_kernel, tm=tm),
        grid=(b, l // tm),
        in_specs=[tile] + specs,
        out_specs=tile,
        out_shape=jax.ShapeDtypeStruct(h.shape, h.dtype),
        scratch_shapes=[blocked((LANE_BLOCKS, CHUNK + tm, V7X_LANES)),
                        blocked((LANE_BLOCKS, tm, V7X_LANES)),
                        blocked((LANE_BLOCKS, tm, V7X_LANES)),
                        blocked((LANE_BLOCKS, tm, V7X_LANES)),
                        pltpu.VMEM(w1.shape[1:], BF16),
                        pltpu.VMEM(w2.shape[1:], BF16)],
        compiler_params=_params(2),
        name="conformer_conv",
    )(h, *args)


DT_WIDTH = V7X_LANES
TAIL_GROUPS = SSM_CONV - 1


def _split_pack(v, lane):
    hi = v.astype(BF16)
    r1 = v - hi.astype(F32)
    mid = r1.astype(BF16)
    lo = (r1 - mid.astype(F32)).astype(BF16)
    zero = jnp.zeros_like(hi)
    return jnp.where(lane < N_HEADS, hi,
                     jnp.where(lane < 2 * N_HEADS, mid,
                               jnp.where(lane < 3 * N_HEADS, lo, zero)))


def _chunk_cumsum(a_q, sub):
    rows = [a_q[v * V7X_SUBLANES:(v + 1) * V7X_SUBLANES, :] for v in range(ROW_GROUPS)]
    d = 1
    while d < ROW_GROUPS:
        wrapped = [_shift_down(rows[v], 1, sub) for v in range(ROW_GROUPS - d, ROW_GROUPS)]
        rows = [rows[v] + (rows[v - d] if v >= d else wrapped[v]) for v in range(ROW_GROUPS)]
        d *= 2
    j = 1
    while j < V7X_SUBLANES:
        rows = [r + _shift_down(r, j, sub) for r in rows]
        j *= 2
    return jnp.concatenate(rows, axis=0)


def _ssm_kernel(h_ref, g_ref, wzx_ref, wdt_ref, cw_ref, cb_ref, dtb_ref, alog_ref, dsk_ref,
                gn_ref, wout_ref, e64_ref, o_ref, tail_ref, st_ref, y_ref, *, tm):
    nblk = tm // CHUNK

    @pl.when(pl.program_id(1) == 0)
    def _():
        tail_ref[...] = jnp.zeros(tail_ref.shape, F32)
        st_ref[...] = jnp.zeros(st_ref.shape, F32)

    x = h_ref[...]
    xn = _rms_scale(x, g_ref[...]).astype(BF16)
    zx = _dot(xn, wzx_ref[...])
    z = zx[:, :D_INNER]
    u = zx[:, D_INNER:]

    sub_c = lax.broadcasted_iota(jnp.int32, (V7X_SUBLANES, CONV_DIM), 0)
    prev_tail = [tail_ref[i * V7X_SUBLANES:(i + 1) * V7X_SUBLANES, :] for i in range(TAIL_GROUPS)]
    conv_rows = []
    for c in range(nblk):
        grp = [u[c * CHUNK + v * V7X_SUBLANES:c * CHUNK + (v + 1) * V7X_SUBLANES, :]
               for v in range(ROW_GROUPS)]
        tail = grp[ROW_GROUPS - TAIL_GROUPS:]
        wrapped = [jnp.where(sub_c >= 1, pltpu.roll(tail[i], 1, 0), pltpu.roll(prev_tail[i], 1, 0))
                   for i in range(TAIL_GROUPS)]
        for v in range(ROW_GROUPS):
            acc = cb_ref[...]
            for d in range(SSM_CONV):
                src = grp[v - d] if v >= d else wrapped[v - d + TAIL_GROUPS]
                acc = acc + cw_ref[SSM_CONV - 1 - d:SSM_CONV - d, :] * src
            conv_rows.append(acc)
        prev_tail = tail
    for i in range(TAIL_GROUPS):
        tail_ref[i * V7X_SUBLANES:(i + 1) * V7X_SUBLANES, :] = prev_tail[i]
    xbc = _silu(jnp.concatenate(conv_rows, axis=0))
    xs = xbc[:, :D_INNER]
    bm = xbc[:, D_INNER:D_INNER + N_GROUPS * D_STATE]
    cm = xbc[:, D_INNER + N_GROUPS * D_STATE:]

    dtr = _dot(xn, wdt_ref[...]) + dtb_ref[...]
    dt = jnp.maximum(dtr, 0.0) + jnp.log(1.0 + jnp.exp(-jnp.abs(dtr)))
    a = dt * (-jnp.exp(alog_ref[...]))

    sub = lax.broadcasted_iota(jnp.int32, (V7X_SUBLANES, DT_WIDTH), 0)
    lane = lax.broadcasted_iota(jnp.int32, (CHUNK, V7X_LANES), 1)
    low = lane < HEAD_DIM
    low1 = lax.broadcasted_iota(jnp.int32, (1, V7X_LANES), 1) < HEAD_DIM
    lo64 = lax.broadcasted_iota(jnp.int32, (HALF_CHUNK, V7X_LANES), 1) < HEAD_DIM
    pos_r = lax.broadcasted_iota(jnp.int32, (CHUNK, 2 * CHUNK), 0)
    kcol = lax.broadcasted_iota(jnp.int32, (CHUNK, 2 * CHUNK), 1)
    pos_c = HALF_CHUNK * (kcol // CHUNK) + (kcol % HALF_CHUNK)
    time_r = ROW_GROUPS * (pos_r % V7X_SUBLANES) + pos_r // V7X_SUBLANES
    time_c = ROW_GROUPS * (pos_c % V7X_SUBLANES) + pos_c // V7X_SUBLANES
    causal = time_c <= time_r
    e64 = e64_ref[...]

    for q in range(nblk):
        r0 = q * CHUNK
        acs = _chunk_cumsum(a[r0:r0 + CHUNK, :], sub)
        acs_t = acs.T
        acs_tr = pltpu.roll(acs_t, HALF_CHUNK, 1)
        packed = jnp.concatenate([_split_pack(dt[r0:r0 + CHUNK, :], lane),
                                  _split_pack(acs, lane)], axis=0)
        spread = _dot(packed, e64)
        dt64 = spread[:CHUNK]
        acs64 = spread[CHUNK:]
        last64 = acs64[CHUNK - 1:CHUNK, :]
        xdt = xs[r0:r0 + CHUNK, :] * dt64
        xdec = (xdt * jnp.exp(last64 - acs64)).astype(BF16)
        dec_out = jnp.exp(acs64)
        dec_tot = jnp.exp(last64)

        for g in range(N_GROUPS):
            gs = slice(g * GROUP_WIDTH, (g + 1) * GROUP_WIDTH)
            b_gt = bm[r0:r0 + CHUNK, g * D_STATE:(g + 1) * D_STATE].T
            c_g = cm[r0:r0 + CHUNK, g * D_STATE:(g + 1) * D_STATE].astype(BF16)
            b_gtr = pltpu.roll(b_gt, HALF_CHUNK, 1)
            b_dup = jnp.concatenate([jnp.where(low, b_gt, b_gtr),
                                     jnp.where(low, b_gtr, b_gt)], axis=1).astype(BF16)
            cb = _dot(c_g, b_dup)
            st = st_ref[g]
            y_off = _dot(c_g, st.astype(BF16)) * dec_out[:, gs]
            st_ref[g] = st * dec_tot[:, gs] + _dot(b_gt.astype(BF16), xdec[:, gs])

            for jj in range(PAIRS_PER_GROUP):
                pair = g * PAIRS_PER_GROUP + jj
                h0, h1 = 2 * pair, 2 * pair + 1
                ps = slice(pair * PAIR_WIDTH, (pair + 1) * PAIR_WIDTH)
                col = acs64[:, ps]
                row_a = jnp.where(low1, acs_t[h0:h0 + 1, :], acs_tr[h1:h1 + 1, :])
                row_b = jnp.where(low1, acs_tr[h0:h0 + 1, :], acs_t[h1:h1 + 1, :])
                seg = jnp.concatenate([col - row_a, col - row_b], axis=1)
                m = (cb * jnp.exp(jnp.where(causal, seg, MASKED_LOG))).astype(BF16)
                xp = xdt[:, ps]
                top, bot = xp[:HALF_CHUNK], xp[HALF_CHUNK:]
                rhs = jnp.concatenate([jnp.where(lo64, top, 0.0), jnp.where(lo64, 0.0, top),
                                       jnp.where(lo64, bot, 0.0), jnp.where(lo64, 0.0, bot)],
                                      axis=0).astype(BF16)
                y_ref[r0:r0 + CHUNK, ps] = (
                    _dot(m, rhs) + y_off[:, jj * PAIR_WIDTH:(jj + 1) * PAIR_WIDTH])
    y = y_ref[...] + xs * dsk_ref[...]
    hg = y * _silu(z)
    normed = []
    for g in range(N_GROUPS):
        hs = hg[:, g * GROUP_WIDTH:(g + 1) * GROUP_WIDTH]
        ms = jnp.mean(hs * hs, axis=-1, keepdims=True)
        normed.append(hs * lax.rsqrt(ms + EPS))
    hn = (jnp.concatenate(normed, axis=1) * gn_ref[...]).astype(BF16)
    o_ref[...] = x + _dot(hn, wout_ref[...])


def _expander(width):
    e = np.zeros((DT_WIDTH, N_HEADS * width), np.float32)
    for r in range(HEAD_REPLICAS):
        for h in range(N_HEADS):
            e[r * N_HEADS + h, h * width:(h + 1) * width] = 1.0
    return jnp.asarray(e, BF16)


def _replicate_heads(v):
    rep = jnp.tile(v, HEAD_REPLICAS)
    return jnp.pad(rep, (0, DT_WIDTH - rep.shape[0]))[None, :]


def _ssm_layer(h, layer, g, w_in_bf, w_in, conv_w, conv_b, dt_bias, a_log, d_skip, gate_norm,
               w_out_bf, *, tm):
    b, l, d = h.shape
    n_zx = D_INNER + CONV_DIM
    wdt = jnp.pad(jnp.tile(w_in[:, n_zx:], (1, HEAD_REPLICAS)),
                  ((0, 0), (0, DT_WIDTH - HEAD_REPLICAS * N_HEADS))).astype(BF16)
    args = (g[None, :], w_in_bf, wdt, conv_w, conv_b[None, :], _replicate_heads(dt_bias),
            _replicate_heads(a_log), jnp.repeat(d_skip, HEAD_DIM)[None, :], gate_norm[None, :],
            w_out_bf, _expander(HEAD_DIM))
    specs = [_resident(a.shape) for a in args]
    specs[1] = _layer_of(w_in_bf, layer, (d, n_zx))
    specs[9] = _layer_of(w_out_bf, layer)
    tile = pl.BlockSpec((None, tm, d), lambda i, j: (i, j, 0))
    return pl.pallas_call(
        functools.partial(_ssm_kernel, tm=tm),
        grid=(b, l // tm),
        in_specs=[tile] + specs,
        out_specs=tile,
        out_shape=jax.ShapeDtypeStruct(h.shape, h.dtype),
        scratch_shapes=[pltpu.VMEM((TAIL_GROUPS * V7X_SUBLANES, CONV_DIM), F32),
                        pltpu.VMEM((N_GROUPS, D_STATE, GROUP_WIDTH), F32),
                        pltpu.VMEM((tm, D_INNER), F32)],
        compiler_params=_params(2),
        name="ssd_mixer",
    )(h, *args)


SSM_TILE = 256
CONF_TILE = 512
FFN_TILE = 512


def kernel(x, ssm_norm, ssm_w_in, ssm_conv_w, ssm_conv_b, ssm_dt_bias, ssm_a_log, ssm_d, ssm_gate_norm, ssm_w_out, cv_norm, cv_w_pw1, cv_b_pw1, cv_dw_w, cv_dw_b, cv_ln_g, cv_ln_b, cv_w_pw2, cv_b_pw2, ffn_norm, ffn_w_gate, ffn_w_up, ffn_w_down, final_norm):
    w_in, w_out = _to_bf16(ssm_w_in), _to_bf16(ssm_w_out)
    w_gate, w_up, w_down = _to_bf16(ffn_w_gate), _to_bf16(ffn_w_up), _to_bf16(ffn_w_down)
    h = _interleave(x)
    for i in range(DEPTH):
        j = i // 2
        if i % 2 == 0:
            h = _ssm_layer(h, j, ssm_norm[j], w_in, ssm_w_in[j], ssm_conv_w[j], ssm_conv_b[j],
                           ssm_dt_bias[j], ssm_a_log[j], ssm_d[j], ssm_gate_norm[j], w_out,
                           tm=SSM_TILE)
        else:
            h = _conf_layer(h, j, cv_norm[j][None, :], cv_w_pw1, cv_b_pw1[j][None, :], cv_dw_w[j],
                            cv_dw_b[j][None, :], cv_ln_g[j][None, :], cv_ln_b[j][None, :],
                            cv_w_pw2, cv_b_pw2[j][None, :], tm=CONF_TILE)
        h = _ffn_layer(h, i, ffn_norm[i][None, :], w_gate, w_up, w_down, final_norm[None, :],
                       final_norm=(i == DEPTH - 1), tm=FFN_TILE)
    return _deinterleave(h)
```

```python
import functools

import numpy as np
import jax
import jax.numpy as jnp
from jax import lax
from jax.experimental import pallas as pl
from jax.experimental.pallas import tpu as pltpu

D_MODEL = 1024
DEPTH = 4
D_INNER = 2048
HEAD_DIM = 64
N_HEADS = 32
N_GROUPS = 4
HEADS_PER_GROUP = 8
D_STATE = 128
SSM_CONV = 4
CONV_DIM = D_INNER + 2 * N_GROUPS * D_STATE
CHUNK = 128
CONV_KERNEL = 31
D_FF = 2816
EPS = 1e-5

V7X_LANES = 128
V7X_SUBLANES = 8
V7X_VMEM_LIMIT_BYTES = 56 * 1024 * 1024

ROW_GROUPS = CHUNK // V7X_SUBLANES
GROUP_WIDTH = HEADS_PER_GROUP * HEAD_DIM
HEAD_REPLICAS = 3
PAIR_WIDTH = 2 * HEAD_DIM
PAIRS_PER_GROUP = HEADS_PER_GROUP // 2
HALF_CHUNK = CHUNK // 2
MASKED_LOG = -1e30
LANE_BLOCKS = D_MODEL // V7X_LANES
CAST_BLOCK_BYTES = 12 * 1024 * 1024
CAST_ROW_ALIGN = 16

F32 = jnp.float32
BF16 = jnp.bfloat16


def _sigmoid(x):
    return 1.0 / (1.0 + jnp.exp(-x))


def _silu(x):
    return x * _sigmoid(x)


def _rms_scale(x, g):
    ms = jnp.mean(x * x, axis=-1, keepdims=True)
    return x * lax.rsqrt(ms + EPS) * g


def _dot(a, b):
    return jnp.dot(a, b, preferred_element_type=F32)


def _resident(shape):
    nd = len(shape)
    return pl.BlockSpec(shape, lambda *_: (0,) * nd, pipeline_mode=pl.Buffered(1))


def _layer_of(stacked, layer, shape=None):
    shape = tuple(stacked.shape[1:]) if shape is None else shape
    nd = len(shape)
    return pl.BlockSpec((None,) + shape, lambda *_: (layer,) + (0,) * nd,
                        pipeline_mode=pl.Buffered(1))


def _cast_kernel(x_ref, o_ref):
    o_ref[...] = x_ref[...].astype(o_ref.dtype)


def _to_bf16(w):
    n, r, c = w.shape
    parts = 1
    while r % parts or (r // parts) % CAST_ROW_ALIGN or (r // parts) * c * 4 > CAST_BLOCK_BYTES:
        parts += 1
    rows = r // parts
    spec = pl.BlockSpec((None, rows, c), lambda i, j: (i, j, 0))
    return pl.pallas_call(
        _cast_kernel,
        grid=(n, parts),
        in_specs=[spec],
        out_specs=spec,
        out_shape=jax.ShapeDtypeStruct(w.shape, BF16),
        compiler_params=_params(2),
        name="cast_bf16",
    )(w)


def _params(n_grid):
    return pltpu.CompilerParams(
        dimension_semantics=("arbitrary",) * n_grid,
        vmem_limit_bytes=V7X_VMEM_LIMIT_BYTES)


def _interleave(x):
    b, l, d = x.shape
    x = x.reshape(b, l // CHUNK, V7X_SUBLANES, ROW_GROUPS, d)
    return x.swapaxes(2, 3).reshape(b, l, d)


def _deinterleave(x):
    b, l, d = x.shape
    x = x.reshape(b, l // CHUNK, ROW_GROUPS, V7X_SUBLANES, d)
    return x.swapaxes(2, 3).reshape(b, l, d)


def _shift_down(x, j, sub):
    return jnp.where(sub >= j, pltpu.roll(x, j, 0), 0.0)


def _ffn_kernel(h_ref, g_ref, wg_ref, wu_ref, wd_ref, fg_ref, o_ref, *, final_norm):
    x = h_ref[...]
    xn = _rms_scale(x, g_ref[...]).astype(BF16)
    gate = _dot(xn, wg_ref[...])
    up = _dot(xn, wu_ref[...])
    act = (_silu(gate) * up).astype(BF16)
    y = x + _dot(act, wd_ref[...])
    if final_norm:
        y = _rms_scale(y, fg_ref[...])
    o_ref[...] = y


def _ffn_layer(h, layer, g, wg, wu, wd, fg, *, final_norm, tm):
    b, l, d = h.shape
    tile = pl.BlockSpec((None, tm, d), lambda i, j: (i, j, 0))
    return pl.pallas_call(
        functools.partial(_ffn_kernel, final_norm=final_norm),
        grid=(b, l // tm),
        in_specs=[tile, _resident(g.shape), _layer_of(wg, layer), _layer_of(wu, layer),
                  _layer_of(wd, layer), _resident(fg.shape)],
        out_specs=tile,
        out_shape=jax.ShapeDtypeStruct(h.shape, h.dtype),
        compiler_params=_params(2),
        name="swiglu_ffn",
    )(h, g, wg, wu, wd, fg)


def _conf_kernel(h_ref, g_ref, w1_ref, b1_ref, dww_ref, dwb_ref, lng_ref, lnb_ref,
                 w2_ref, b2_ref, o_ref, u0_ref, u1_ref, u2_ref, acc_ref, w1b_ref, w2b_ref,
                 *, tm):
    nblk = tm // CHUNK

    @pl.when((pl.program_id(0) == 0) & (pl.program_id(1) == 0))
    def _():
        w1b_ref[...] = w1_ref[...].astype(BF16)
        w2b_ref[...] = w2_ref[...].astype(BF16)

    @pl.when(pl.program_id(1) == 0)
    def _():
        u0_ref[:, 0:CHUNK, :] = jnp.zeros((LANE_BLOCKS, CHUNK, V7X_LANES), F32)

    x = h_ref[...]
    xn = _rms_scale(x, g_ref[...]).astype(BF16)
    hh = _dot(xn, w1b_ref[...]) + b1_ref[...]
    u = hh[:, :D_MODEL] * _sigmoid(hh[:, D_MODEL:])
    for lb in range(LANE_BLOCKS):
        u0_ref[lb, CHUNK:CHUNK + tm, :] = u[:, lb * V7X_LANES:(lb + 1) * V7X_LANES]

    sub = lax.broadcasted_iota(jnp.int32, (V7X_SUBLANES, V7X_LANES), 0)

    def lane_block(lb, carry):
        def group(ref, c, v):
            r = c * CHUNK + v * V7X_SUBLANES
            return ref[lb, r:r + V7X_SUBLANES, :]

        prev1 = [pltpu.roll(group(u0_ref, 0, v), 1, 0) for v in range(ROW_GROUPS)]
        prev2 = [pltpu.roll(group(u0_ref, 0, v), 2, 0) for v in range(ROW_GROUPS)]
        for c in range(nblk):
            for v in range(ROW_GROUPS):
                cur = group(u0_ref, c + 1, v)
                r1 = pltpu.roll(cur, 1, 0)
                r2 = pltpu.roll(cur, 2, 0)
                r = c * CHUNK + v * V7X_SUBLANES
                u1_ref[lb, r:r + V7X_SUBLANES, :] = jnp.where(sub >= 1, r1, prev1[v])
                u2_ref[lb, r:r + V7X_SUBLANES, :] = jnp.where(sub >= 2, r2, prev2[v])
                prev1[v] = r1
                prev2[v] = r2

        w = [jnp.broadcast_to(dww_ref[lb, k:k + 1, :], (V7X_SUBLANES, V7X_LANES))
             for k in range(CONV_KERNEL)]
        bias = jnp.broadcast_to(dwb_ref[lb], (V7X_SUBLANES, V7X_LANES))
        for c in range(nblk):
            for v in range(ROW_GROUPS):
                acc = bias
                for d in range(CONV_KERNEL):
                    a, r = divmod(d, ROW_GROUPS)
                    vv, j = (v - r, a) if v >= r else (v - r + ROW_GROUPS, a + 1)
                    if j == 0:
                        src = group(u0_ref, c + 1, vv)
                    elif j == 1:
                        src = group(u1_ref, c, vv)
                    else:
                        src = group(u2_ref, c, vv)
                    acc = acc + w[CONV_KERNEL - 1 - d] * src
                r = c * CHUNK + v * V7X_SUBLANES
                acc_ref[lb, r:r + V7X_SUBLANES, :] = acc
        return carry

    lax.fori_loop(0, LANE_BLOCKS, lane_block, 0)
    u0_ref[:, 0:CHUNK, :] = u0_ref[:, tm:tm + CHUNK, :]

    conv = jnp.concatenate([acc_ref[lb] for lb in range(LANE_BLOCKS)], axis=1)
    mu = jnp.mean(conv, axis=-1, keepdims=True)
    xc = conv - mu
    var = jnp.mean(xc * xc, axis=-1, keepdims=True)
    y = xc * lax.rsqrt(var + EPS) * lng_ref[...] + lnb_ref[...]
    s = _silu(y).astype(BF16)
    o_ref[...] = x + _dot(s, w2b_ref[...]) + b2_ref[...]


def _lane_blocked(w):
    return w.reshape(w.shape[0], LANE_BLOCKS, V7X_LANES).swapaxes(0, 1)


def _conf_layer(h, layer, g, w1, b1, dww, dwb, lng, lnb, w2, b2, *, tm):
    b, l, d = h.shape
    tile = pl.BlockSpec((None, tm, d), lambda i, j: (i, j, 0))
    args = (g, w1, b1, _lane_blocked(dww), _lane_blocked(dwb), lng, lnb, w2, b2)
    specs = [_layer_of(a, layer) if a is w1 or a is w2 else _resident(a.shape) for a in args]
    blocked = functools.partial(pltpu.VMEM, dtype=F32)
    return pl.pallas_call(
        functools.partial(_conf_kernel, tm=tm),
        grid=(b, l // tm),
        in_specs=[tile] + specs,
        out_specs=tile,
        out_shape=jax.ShapeDtypeStruct(h.shape, h.dtype),
        scratch_shapes=[blocked((LANE_BLOCKS, CHUNK + tm, V7X_LANES)),
                        blocked((LANE_BLOCKS, tm, V7X_LANES)),
                        blocked((LANE_BLOCKS, tm, V7X_LANES)),
                        blocked((LANE_BLOCKS, tm, V7X_LANES)),
                        pltpu.VMEM(w1.shape[1:], BF16),
                        pltpu.VMEM(w2.shape[1:], BF16)],
        compiler_params=_params(2),
        name="conformer_conv",
    )(h, *args)


DT_WIDTH = V7X_LANES
TAIL_GROUPS = SSM_CONV - 1


def _split_pack(v, lane):
    hi = v.astype(BF16)
    r1 = v - hi.astype(F32)
    mid = r1.astype(BF16)
    lo = (r1 - mid.astype(F32)).astype(BF16)
    zero = jnp.zeros_like(hi)
    return jnp.where(lane < N_HEADS, hi,
                     jnp.where(lane < 2 * N_HEADS, mid,
                               jnp.where(lane < 3 * N_HEADS, lo, zero)))


def _chunk_cumsum(a_q, sub):
    rows = [a_q[v * V7X_SUBLANES:(v + 1) * V7X_SUBLANES, :] for v in range(ROW_GROUPS)]
    d = 1
    while d < ROW_GROUPS:
        wrapped = [_shift_down(rows[v], 1, sub) for v in range(ROW_GROUPS - d, ROW_GROUPS)]
        rows = [rows[v] + (rows[v - d] if v >= d else wrapped[v]) for v in range(ROW_GROUPS)]
        d *= 2
    j = 1
    while j < V7X_SUBLANES:
        rows = [r + _shift_down(r, j, sub) for r in rows]
        j *= 2
    return jnp.concatenate(rows, axis=0)


def _ssm_kernel(h_ref, g_ref, wzx_ref, wdt_ref, cw_ref, cb_ref, dtb_ref, alog_ref, dsk_ref,
                gn_ref, wout_ref, e64_ref, o_ref, tail_ref, st_ref, y_ref, woutb_ref, *, tm):
    nblk = tm // CHUNK

    @pl.when((pl.program_id(0) == 0) & (pl.program_id(1) == 0))
    def _():
        woutb_ref[...] = wout_ref[...].astype(BF16)

    @pl.when(pl.program_id(1) == 0)
    def _():
        tail_ref[...] = jnp.zeros(tail_ref.shape, F32)
        st_ref[...] = jnp.zeros(st_ref.shape, F32)

    x = h_ref[...]
    xn = _rms_scale(x, g_ref[...]).astype(BF16)
    zx = _dot(xn, wzx_ref[...])
    z = zx[:, :D_INNER]
    u = zx[:, D_INNER:]

    sub_c = lax.broadcasted_iota(jnp.int32, (V7X_SUBLANES, CONV_DIM), 0)
    prev_tail = [tail_ref[i * V7X_SUBLANES:(i + 1) * V7X_SUBLANES, :] for i in range(TAIL_GROUPS)]
    conv_rows = []
    for c in range(nblk):
        grp = [u[c * CHUNK + v * V7X_SUBLANES:c * CHUNK + (v + 1) * V7X_SUBLANES, :]
               for v in range(ROW_GROUPS)]
        tail = grp[ROW_GROUPS - TAIL_GROUPS:]
        wrapped = [jnp.where(sub_c >= 1, pltpu.roll(tail[i], 1, 0), pltpu.roll(prev_tail[i], 1, 0))
                   for i in range(TAIL_GROUPS)]
        for v in range(ROW_GROUPS):
            acc = cb_ref[...]
            for d in range(SSM_CONV):
                src = grp[v - d] if v >= d else wrapped[v - d + TAIL_GROUPS]
                acc = acc + cw_ref[SSM_CONV - 1 - d:SSM_CONV - d, :] * src
            conv_rows.append(acc)
        prev_tail = tail
    for i in range(TAIL_GROUPS):
        tail_ref[i * V7X_SUBLANES:(i + 1) * V7X_SUBLANES, :] = prev_tail[i]
    xbc = _silu(jnp.concatenate(conv_rows, axis=0))
    xs = xbc[:, :D_INNER]
    bm = xbc[:, D_INNER:D_INNER + N_GROUPS * D_STATE]
    cm = xbc[:, D_INNER + N_GROUPS * D_STATE:]

    dtr = _dot(xn, wdt_ref[...]) + dtb_ref[...]
    dt = jnp.maximum(dtr, 0.0) + jnp.log(1.0 + jnp.exp(-jnp.abs(dtr)))
    a = dt * (-jnp.exp(alog_ref[...]))

    sub = lax.broadcasted_iota(jnp.int32, (V7X_SUBLANES, DT_WIDTH), 0)
    lane = lax.broadcasted_iota(jnp.int32, (CHUNK, V7X_LANES), 1)
    low = lane < HEAD_DIM
    low1 = lax.broadcasted_iota(jnp.int32, (1, V7X_LANES), 1) < HEAD_DIM
    lo64 = lax.broadcasted_iota(jnp.int32, (HALF_CHUNK, V7X_LANES), 1) < HEAD_DIM
    pos_r = lax.broadcasted_iota(jnp.int32, (CHUNK, 2 * CHUNK), 0)
    kcol = lax.broadcasted_iota(jnp.int32, (CHUNK, 2 * CHUNK), 1)
    pos_c = HALF_CHUNK * (kcol // CHUNK) + (kcol % HALF_CHUNK)
    time_r = ROW_GROUPS * (pos_r % V7X_SUBLANES) + pos_r // V7X_SUBLANES
    time_c = ROW_GROUPS * (pos_c % V7X_SUBLANES) + pos_c // V7X_SUBLANES
    causal = time_c <= time_r
    e64 = e64_ref[...]

    for q in range(nblk):
        r0 = q * CHUNK
        acs = _chunk_cumsum(a[r0:r0 + CHUNK, :], sub)
        acs_t = acs.T
        acs_tr = pltpu.roll(acs_t, HALF_CHUNK, 1)
        packed = jnp.concatenate([_split_pack(dt[r0:r0 + CHUNK, :], lane),
                                  _split_pack(acs, lane)], axis=0)
        spread = _dot(packed, e64)
        dt64 = spread[:CHUNK]
        acs64 = spread[CHUNK:]
        last64 = acs64[CHUNK - 1:CHUNK, :]
        xdt = xs[r0:r0 + CHUNK, :] * dt64
        xdec = (xdt * jnp.exp(last64 - acs64)).astype(BF16)
        dec_out = jnp.exp(acs64)
        dec_tot = jnp.exp(last64)

        for g in range(N_GROUPS):
            gs = slice(g * GROUP_WIDTH, (g + 1) * GROUP_WIDTH)
            b_gt = bm[r0:r0 + CHUNK, g * D_STATE:(g + 1) * D_STATE].T
            c_g = cm[r0:r0 + CHUNK, g * D_STATE:(g + 1) * D_STATE].astype(BF16)
            b_gtr = pltpu.roll(b_gt, HALF_CHUNK, 1)
            b_dup = jnp.concatenate([jnp.where(low, b_gt, b_gtr),
                                     jnp.where(low, b_gtr, b_gt)], axis=1).astype(BF16)
            cb = _dot(c_g, b_dup)
            st = st_ref[g]
            y_off = _dot(c_g, st.astype(BF16)) * dec_out[:, gs]
            st_ref[g] = st * dec_tot[:, gs] + _dot(b_gt.astype(BF16), xdec[:, gs])

            for jj in range(PAIRS_PER_GROUP):
                pair = g * PAIRS_PER_GROUP + jj
                h0, h1 = 2 * pair, 2 * pair + 1
                ps = slice(pair * PAIR_WIDTH, (pair + 1) * PAIR_WIDTH)
                col = acs64[:, ps]
                row_a = jnp.where(low1, acs_t[h0:h0 + 1, :], acs_tr[h1:h1 + 1, :])
                row_b = jnp.where(low1, acs_tr[h0:h0 + 1, :], acs_t[h1:h1 + 1, :])
                seg = jnp.concatenate([col - row_a, col - row_b], axis=1)
                m = (cb * jnp.exp(jnp.where(causal, seg, MASKED_LOG))).astype(BF16)
                xp = xdt[:, ps]
                top, bot = xp[:HALF_CHUNK], xp[HALF_CHUNK:]
                rhs = jnp.concatenate([jnp.where(lo64, top, 0.0), jnp.where(lo64, 0.0, top),
                                       jnp.where(lo64, bot, 0.0), jnp.where(lo64, 0.0, bot)],
                                      axis=0).astype(BF16)
                y_ref[r0:r0 + CHUNK, ps] = (
                    _dot(m, rhs) + y_off[:, jj * PAIR_WIDTH:(jj + 1) * PAIR_WIDTH])
    y = y_ref[...] + xs * dsk_ref[...]
    hg = y * _silu(z)
    normed = []
    for g in range(N_GROUPS):
        hs = hg[:, g * GROUP_WIDTH:(g + 1) * GROUP_WIDTH]
        ms = jnp.mean(hs * hs, axis=-1, keepdims=True)
        normed.append(hs * lax.rsqrt(ms + EPS))
    hn = (jnp.concatenate(normed, axis=1) * gn_ref[...]).astype(BF16)
    o_ref[...] = x + _dot(hn, woutb_ref[...])


def _expander(width):
    e = np.zeros((DT_WIDTH, N_HEADS * width), np.float32)
    for r in range(HEAD_REPLICAS):
        for h in range(N_HEADS):
            e[r * N_HEADS + h, h * width:(h + 1) * width] = 1.0
    return jnp.asarray(e, BF16)


def _replicate_heads(v):
    rep = jnp.tile(v, HEAD_REPLICAS)
    return jnp.pad(rep, (0, DT_WIDTH - rep.shape[0]))[None, :]


def _ssm_layer(h, layer, g, w_in_bf, w_in, conv_w, conv_b, dt_bias, a_log, d_skip, gate_norm,
               w_out_bf, *, tm):
    b, l, d = h.shape
    n_zx = D_INNER + CONV_DIM
    wdt = jnp.pad(jnp.tile(w_in[:, n_zx:], (1, HEAD_REPLICAS)),
                  ((0, 0), (0, DT_WIDTH - HEAD_REPLICAS * N_HEADS))).astype(BF16)
    args = (g[None, :], w_in_bf, wdt, conv_w, conv_b[None, :], _replicate_heads(dt_bias),
            _replicate_heads(a_log), jnp.repeat(d_skip, HEAD_DIM)[None, :], gate_norm[None, :],
            w_out_bf, _expander(HEAD_DIM))
    specs = [_resident(a.shape) for a in args]
    specs[1] = _layer_of(w_in_bf, layer, (d, n_zx))
    specs[9] = _layer_of(w_out_bf, layer)
    tile = pl.BlockSpec((None, tm, d), lambda i, j: (i, j, 0))
    return pl.pallas_call(
        functools.partial(_ssm_kernel, tm=tm),
        grid=(b, l // tm),
        in_specs=[tile] + specs,
        out_specs=tile,
        out_shape=jax.ShapeDtypeStruct(h.shape, h.dtype),
        scratch_shapes=[pltpu.VMEM((TAIL_GROUPS * V7X_SUBLANES, CONV_DIM), F32),
                        pltpu.VMEM((N_GROUPS, D_STATE, GROUP_WIDTH), F32),
                        pltpu.VMEM((tm, D_INNER), F32),
                        pltpu.VMEM(w_out_bf.shape[1:], BF16)],
        compiler_params=_params(2),
        name="ssd_mixer",
    )(h, *args)


SSM_TILE = 256
CONF_TILE = 512
FFN_TILE = 512


def kernel(x, ssm_norm, ssm_w_in, ssm_conv_w, ssm_conv_b, ssm_dt_bias, ssm_a_log, ssm_d, ssm_gate_norm, ssm_w_out, cv_norm, cv_w_pw1, cv_b_pw1, cv_dw_w, cv_dw_b, cv_ln_g, cv_ln_b, cv_w_pw2, cv_b_pw2, ffn_norm, ffn_w_gate, ffn_w_up, ffn_w_down, final_norm):
    w_in, w_out = _to_bf16(ssm_w_in), ssm_w_out
    w_gate, w_up, w_down = _to_bf16(ffn_w_gate), _to_bf16(ffn_w_up), _to_bf16(ffn_w_down)
    h = _interleave(x)
    for i in range(DEPTH):
        j = i // 2
        if i % 2 == 0:
            h = _ssm_layer(h, j, ssm_norm[j], w_in, ssm_w_in[j], ssm_conv_w[j], ssm_conv_b[j],
                           ssm_dt_bias[j], ssm_a_log[j], ssm_d[j], ssm_gate_norm[j], w_out,
                           tm=SSM_TILE)
        else:
            h = _conf_layer(h, j, cv_norm[j][None, :], cv_w_pw1, cv_b_pw1[j][None, :], cv_dw_w[j],
                            cv_dw_b[j][None, :], cv_ln_g[j][None, :], cv_ln_b[j][None, :],
                            cv_w_pw2, cv_b_pw2[j][None, :], tm=CONF_TILE)
        h = _ffn_layer(h, i, ffn_norm[i][None, :], w_gate, w_up, w_down, final_norm[None, :],
                       final_norm=(i == DEPTH - 1), tm=FFN_TILE)
    return _deinterleave(h)
```
